```python
import math
import jax, jax.numpy as jnp
from jax import lax
import numpy as np

D_MODEL = 1024
BATCH = 8
SEQ = 2048
DEPTH = 4

CHUNK = 64
N_MEM = 256
N_A_LAYERS = DEPTH // 2
N_B_LAYERS = DEPTH - N_A_LAYERS
MEM_HEADS = 4
MEM_HEAD_DIM = 64
MEM_WIDTH = MEM_HEADS * MEM_HEAD_DIM
TOK_WIDTH = D_MODEL - MEM_WIDTH
SSM_GROUP = 16
SSM_GROUPS = TOK_WIDTH // SSM_GROUP
SSM_STATE = 64
DIFF_HEAD_DIM = 64
DIFF_HEADS = TOK_WIDTH // (2 * DIFF_HEAD_DIM)
DIFF_V_DIM = 2 * DIFF_HEAD_DIM
D_FF = ((math.ceil(8 * D_MODEL / 3) + 255) // 256) * 256
Q_BLOCK = 128
EPS = 1e-6
DT_MIN = 0.001
DT_MAX = 0.1

kernel_name = "hybrid_s5_diffattn_yoco_encoder"


def rms_norm(x, g):
    xf = x.astype(jnp.float32)
    y = xf * lax.rsqrt(jnp.mean(xf * xf, axis=-1, keepdims=True) + EPS)
    return (y * g.astype(jnp.float32)).astype(x.dtype)


def s5_mixer(u, a_re, a_im, log_dt, b_re, b_im, c_re, c_im, d_skip, w_glu):
    bsz, L, _ = u.shape
    f32 = jnp.float32
    uf = u.astype(f32).reshape(bsz, L, SSM_GROUPS, SSM_GROUP)
    a_re = a_re.astype(f32); a_im = a_im.astype(f32)
    dt = jnp.exp(log_dt.astype(f32))[:, None]
    mag = jnp.exp(dt * a_re)
    abar_re = mag * jnp.cos(dt * a_im)
    abar_im = mag * jnp.sin(dt * a_im)
    den = a_re * a_re + a_im * a_im
    nr = abar_re - 1.0
    coef_re = (nr * a_re + abar_im * a_im) / den
    coef_im = (abar_im * a_re - nr * a_im) / den
    b_re = b_re.astype(f32); b_im = b_im.astype(f32)
    bbar_re = coef_re[..., None] * b_re - coef_im[..., None] * b_im
    bbar_im = coef_re[..., None] * b_im + coef_im[..., None] * b_re
    bu_re = jnp.einsum('blgp,gnp->blgn', uf, bbar_re)
    bu_im = jnp.einsum('blgp,gnp->blgn', uf, bbar_im)
    a_seq_re = jnp.broadcast_to(abar_re, (1, L, SSM_GROUPS, SSM_STATE))
    a_seq_im = jnp.broadcast_to(abar_im, (1, L, SSM_GROUPS, SSM_STATE))

    def combine(left, right):
        al_re, al_im, bl_re, bl_im = left
        ar_re, ar_im, br_re, br_im = right
        return (al_re * ar_re - al_im * ar_im,
                al_re * ar_im + al_im * ar_re,
                ar_re * bl_re - ar_im * bl_im + br_re,
                ar_re * bl_im + ar_im * bl_re + br_im)

    _, _, s_re, s_im = lax.associative_scan(combine, (a_seq_re, a_seq_im, bu_re, bu_im), axis=1)
    y = (jnp.einsum('blgn,gpn->blgp', s_re, c_re.astype(f32))
         - jnp.einsum('blgn,gpn->blgp', s_im, c_im.astype(f32)))
    y = y.reshape(bsz, L, TOK_WIDTH) + d_skip.astype(f32) * uf.reshape(bsz, L, TOK_WIDTH)
    h = jax.nn.gelu(y).astype(u.dtype)
    return h * jax.nn.sigmoid(h @ w_glu)


def memory_attention(q, mem_n, wk, wv, q_g, k_g):
    bsz, L, _ = q.shape
    M = mem_n.shape[1]
    qh = rms_norm(q.reshape(bsz, L, MEM_HEADS, MEM_HEAD_DIM), q_g)
    kh = rms_norm((mem_n @ wk).reshape(bsz, M, MEM_HEADS, MEM_HEAD_DIM), k_g)
    vh = (mem_n @ wv).reshape(bsz, M, MEM_HEADS, MEM_HEAD_DIM)
    s = jnp.einsum('blhd,bmhd->bhlm', qh, kh).astype(jnp.float32) * (MEM_HEAD_DIM ** -0.5)
    p = jax.nn.softmax(s, axis=-1)
    o = jnp.einsum('bhlm,bmhd->blhd', p.astype(vh.dtype), vh)
    return o.reshape(bsz, L, MEM_WIDTH)


def diff_attention(q, k, v, lam, sub_g, lambda_init):
    bsz, L = q.shape[0], q.shape[1]
    nb = L // Q_BLOCK
    qb = q.reshape(bsz, nb, Q_BLOCK, 2, DIFF_HEADS, DIFF_HEAD_DIM).transpose(1, 0, 2, 3, 4, 5)
    key_chunk = jnp.arange(L) // CHUNK
    scale = DIFF_HEAD_DIM ** -0.5

    def one_block(args):
        qi, idx = args
        q_chunk = (idx * Q_BLOCK + jnp.arange(Q_BLOCK)) // CHUNK
        mask = key_chunk[None, :] <= q_chunk[:, None]
        s = jnp.einsum('bqchd,bkchd->bchqk', qi, k).astype(jnp.float32) * scale
        s = jnp.where(mask, s, -jnp.inf)
        p = jax.nn.softmax(s, axis=-1)
        a = p[:, 0] - lam * p[:, 1]
        return jnp.einsum('bhqk,bkhe->bqhe', a.astype(v.dtype), v)

    out = lax.map(one_block, (qb, jnp.arange(nb)))
    out = out.transpose(1, 0, 2, 3, 4).reshape(bsz, L, DIFF_HEADS, DIFF_V_DIM)
    out = rms_norm(out, sub_g) * (1.0 - lambda_init)
    return out.reshape(bsz, L, TOK_WIDTH)


def swiglu(h, wg, wu, wd):
    return (jax.nn.silu(h @ wg) * (h @ wu)) @ wd


def setup_inputs(seed: int = 0) -> dict:
    key = jax.random.key(seed)
    ks = iter(jax.random.split(key, 40))
    f32 = jnp.float32

    def nrm(shape, scale):
        return jax.random.normal(next(ks), shape, f32) * scale

    def gain(shape):
        return 1.0 + nrm(shape, 0.02)

    G, N, P = SSM_GROUPS, SSM_STATE, SSM_GROUP
    a_im_init = jnp.broadcast_to(jnp.pi * jnp.arange(N, dtype=f32), (N_A_LAYERS, G, N))
    return {
        "x": nrm((BATCH, SEQ, D_MODEL), 1.0),
        "mem": nrm((BATCH, N_MEM, D_MODEL), 1.0),
        "norm_mix": gain((DEPTH, D_MODEL)),
        "norm_ffn": gain((DEPTH, D_MODEL)),
        "norm_mem": gain((DEPTH, D_MODEL)),
        "w_in": nrm((DEPTH, D_MODEL, D_MODEL), D_MODEL ** -0.5),
        "w_out": nrm((DEPTH, D_MODEL, D_MODEL), D_MODEL ** -0.5),
        "mem_wk": nrm((DEPTH, D_MODEL, MEM_WIDTH), D_MODEL ** -0.5),
        "mem_wv": nrm((DEPTH, D_MODEL, MEM_WIDTH), D_MODEL ** -0.5),
        "mem_q_norm": gain((DEPTH, MEM_HEAD_DIM)),
        "mem_k_norm": gain((DEPTH, MEM_HEAD_DIM)),
        "ffn_w_gate": nrm((DEPTH, D_MODEL, D_FF), D_MODEL ** -0.5),
        "ffn_w_up": nrm((DEPTH, D_MODEL, D_FF), D_MODEL ** -0.5),
        "ffn_w_down": nrm((DEPTH, D_FF, D_MODEL), D_FF ** -0.5),
        "ssm_a_re": -0.5 + nrm((N_A_LAYERS, G, N), 0.01),
        "ssm_a_im": a_im_init + nrm((N_A_LAYERS, G, N), 0.01),
        "ssm_log_dt": jax.random.uniform(next(ks), (N_A_LAYERS, G), f32,
                                         minval=math.log(DT_MIN), maxval=math.log(DT_MAX)),
        "ssm_b_re": nrm((N_A_LAYERS, G, N, P), (2 * P) ** -0.5),
        "ssm_b_im": nrm((N_A_LAYERS, G, N, P), (2 * P) ** -0.5),
        "ssm_c_re": nrm((N_A_LAYERS, G, P, N), (2 * N) ** -0.5),
        "ssm_c_im": nrm((N_A_LAYERS, G, P, N), (2 * N) ** -0.5),
        "ssm_d": nrm((N_A_LAYERS, TOK_WIDTH), 0.5),
        "ssm_w_glu": nrm((N_A_LAYERS, TOK_WIDTH, TOK_WIDTH), TOK_WIDTH ** -0.5),
        "kv_norm": gain((D_MODEL,)),
        "diff_wk": nrm((D_MODEL, TOK_WIDTH), D_MODEL ** -0.5),
        "diff_wv": nrm((D_MODEL, TOK_WIDTH), D_MODEL ** -0.5),
        "diff_k_norm": gain((DIFF_HEAD_DIM,)),
        "diff_q_norm": gain((N_B_LAYERS, DIFF_HEAD_DIM)),
        "diff_lambda_q1": nrm((N_B_LAYERS, DIFF_HEAD_DIM), 0.1),
        "diff_lambda_k1": nrm((N_B_LAYERS, DIFF_HEAD_DIM), 0.1),
        "diff_lambda_q2": nrm((N_B_LAYERS, DIFF_HEAD_DIM), 0.1),
        "diff_lambda_k2": nrm((N_B_LAYERS, DIFF_HEAD_DIM), 0.1),
        "diff_sub_norm": gain((N_B_LAYERS, DIFF_V_DIM)),
    }


def reference(x, mem, norm_mix, norm_ffn, norm_mem, w_in, w_out, mem_wk, mem_wv,
              mem_q_norm, mem_k_norm, ffn_w_gate, ffn_w_up, ffn_w_down,
              ssm_a_re, ssm_a_im, ssm_log_dt, ssm_b_re, ssm_b_im, ssm_c_re, ssm_c_im,
              ssm_d, ssm_w_glu, kv_norm, diff_wk, diff_wv, diff_k_norm, diff_q_norm,
              diff_lambda_q1, diff_lambda_k1, diff_lambda_q2, diff_lambda_k2, diff_sub_norm):
    bsz, L, _ = x.shape
    k_shared = None
    v_shared = None
    for layer in range(DEPTH):
        h = rms_norm(x, norm_mix[layer])
        z = h @ w_in[layer]
        z_tok, z_mem = z[..., :TOK_WIDTH], z[..., TOK_WIDTH:]
        mem_n = rms_norm(mem, norm_mem[layer])
        o_mem = memory_attention(z_mem, mem_n, mem_wk[layer], mem_wv[layer],
                                 mem_q_norm[layer], mem_k_norm[layer])
        if layer < N_A_LAYERS:
            i = layer
            o_tok = s5_mixer(z_tok, ssm_a_re[i], ssm_a_im[i], ssm_log_dt[i], ssm_b_re[i],
                             ssm_b_im[i], ssm_c_re[i], ssm_c_im[i], ssm_d[i], ssm_w_glu[i])
        else:
            j = layer - N_A_LAYERS
            lambda_init = 0.8 - 0.6 * math.exp(-0.3 * layer)
            lam = (jnp.exp(jnp.sum(diff_lambda_q1[j].astype(jnp.float32) * diff_lambda_k1[j].astype(jnp.float32)))
                   - jnp.exp(jnp.sum(diff_lambda_q2[j].astype(jnp.float32) * diff_lambda_k2[j].astype(jnp.float32)))
                   + lambda_init)
            q = rms_norm(z_tok.reshape(bsz, L, 2, DIFF_HEADS, DIFF_HEAD_DIM), diff_q_norm[j])
            o_tok = diff_attention(q, k_shared, v_shared, lam, diff_sub_norm[j], lambda_init)
        x = x + jnp.concatenate([o_tok.astype(x.dtype), o_mem.astype(x.dtype)], axis=-1) @ w_out[layer]
        h = rms_norm(x, norm_ffn[layer])
        x = x + swiglu(h, ffn_w_gate[layer], ffn_w_up[layer], ffn_w_down[layer])
        if layer == N_A_LAYERS - 1:
            hk = rms_norm(x, kv_norm)
            k_shared = rms_norm((hk @ diff_wk).reshape(bsz, L, 2, DIFF_HEADS, DIFF_HEAD_DIM), diff_k_norm)
            v_shared = (hk @ diff_wv).reshape(bsz, L, DIFF_HEADS, DIFF_V_DIM)
    return x
```

```python
import functools
import math

import jax
import jax.numpy as jnp
from jax import lax
from jax.experimental import pallas as pl
from jax.experimental.pallas import tpu as pltpu

D_MODEL = 1024
DEPTH = 4
CHUNK = 64
N_A_LAYERS = DEPTH // 2
MEM_HEADS = 4
MEM_HEAD_DIM = 64
MEM_WIDTH = MEM_HEADS * MEM_HEAD_DIM
TOK_WIDTH = D_MODEL - MEM_WIDTH
SSM_P = 16
SSM_G = TOK_WIDTH // SSM_P
SSM_N = 64
SSM_C = 64
SSM_K = SSM_P * SSM_C
DIFF_HEAD_DIM = 64
DIFF_HEADS = TOK_WIDTH // (2 * DIFF_HEAD_DIM)
DIFF_V_DIM = 2 * DIFF_HEAD_DIM
D_FF = 2816
EPS = 1e-6

LANES = 128
VMEM_LIMIT = 56 * 1024 * 1024

F32 = jnp.float32
BF16 = jnp.bfloat16


def _cparams(n_grid, vmem=VMEM_LIMIT):
    return pltpu.CompilerParams(
        dimension_semantics=("arbitrary",) * n_grid, vmem_limit_bytes=vmem)


def _const_spec(shape):
    nd = len(shape)
    return pl.BlockSpec(shape, lambda *_: (0,) * nd, pipeline_mode=pl.Buffered(1))


def _rms(x, g):
    return x * lax.rsqrt(jnp.mean(x * x, axis=-1, keepdims=True) + EPS) * g


def _seg_rms64(x, g):
    rows, width = x.shape
    lane = lax.broadcasted_iota(jnp.int32, (rows, LANES), 1)
    lo = lane < 64
    outs = []
    for v in range(width // LANES):
        xv = x[:, v * LANES:(v + 1) * LANES]
        sq = xv * xv
        s_lo = jnp.sum(jnp.where(lo, sq, 0.0), axis=-1, keepdims=True)
        s_hi = jnp.sum(jnp.where(lo, 0.0, sq), axis=-1, keepdims=True)
        r_lo = lax.rsqrt(s_lo * (1.0 / 64) + EPS)
        r_hi = lax.rsqrt(s_hi * (1.0 / 64) + EPS)
        outs.append(xv * jnp.where(lo, r_lo, r_hi) * g[:, v * LANES:(v + 1) * LANES])
    return jnp.concatenate(outs, axis=1)


def _half_masks(rows, dtype):
    lane = lax.broadcasted_iota(jnp.int32, (rows, LANES), 1)
    lo = (lane < 64).astype(dtype)
    return lo, (1 - lo).astype(dtype)


def _dot(a, b):
    return jnp.dot(a, b, preferred_element_type=F32)


def _dot_nt(a, b):
    return lax.dot_general(a, b, (((1,), (1,)), ((), ())), preferred_element_type=F32)


def _mem_kv_kernel(mem_ref, g_ref, wkv_ref, kg_ref, k_ref, v_ref):
    m = mem_ref[0]
    mn = _rms(m, g_ref[0]).astype(BF16)
    kv = _dot(mn, wkv_ref[0])
    k = _seg_rms64(kv[:, :MEM_WIDTH], kg_ref[0])
    k_ref[0, 0] = k.astype(BF16)
    v_ref[0, 0] = kv[:, MEM_WIDTH:].astype(BF16)


def _mem_kv(mem, norm_mem, wkv, k_gain):
    bsz, n_mem, _ = mem.shape
    out = jax.ShapeDtypeStruct((DEPTH, bsz, n_mem, MEM_WIDTH), BF16)
    return pl.pallas_call(
        _mem_kv_kernel,
        grid=(DEPTH, bsz),
        in_specs=[
            pl.BlockSpec((1, n_mem, D_MODEL), lambda l, b: (b, 0, 0)),
            pl.BlockSpec((1, 1, D_MODEL), lambda l, b: (l, 0, 0)),
            pl.BlockSpec((1, D_MODEL, 2 * MEM_WIDTH), lambda l, b: (l, 0, 0)),
            pl.BlockSpec((1, 1, MEM_WIDTH), lambda l, b: (l, 0, 0)),
        ],
        out_specs=[pl.BlockSpec((1, 1, n_mem, MEM_WIDTH), lambda l, b: (l, b, 0, 0))] * 2,
        out_shape=[out, out],
        compiler_params=_cparams(2),
        name="mem_kv",
    )(mem, norm_mem, wkv, k_gain)


def _mem_attention(z_mem, qg, k, v):
    rows = z_mem.shape[0]
    q = (_seg_rms64(z_mem, qg) * (MEM_HEAD_DIM ** -0.5)).astype(BF16)
    lo_q, hi_q = _half_masks(rows, BF16)
    lo_v, hi_v = _half_masks(v.shape[0], BF16)
    outs = []
    for grp in range(MEM_WIDTH // LANES):
        sl = slice(grp * LANES, (grp + 1) * LANES)
        o = jnp.zeros((rows, LANES), F32)
        for mq, mv in ((lo_q, lo_v), (hi_q, hi_v)):
            s = _dot_nt(q[:, sl] * mq, k[:, sl])
            s = s - jnp.max(s, axis=-1, keepdims=True)
            p = jnp.exp(s)
            p = p / jnp.sum(p, axis=-1, keepdims=True)
            o = o + _dot(p.astype(BF16), v[:, sl] * mv)
        outs.append(o)
    return jnp.concatenate(outs, axis=1)


def _in_proj_kernel(x_ref, g_ref, wtok_ref, wmem_ref, tokg_ref, qg_ref, k_ref, v_ref,
                    ztok_ref, omem_ref, *, transposed_tok, norm_tok):
    h = _rms(x_ref[...], g_ref[...]).astype(BF16)
    if transposed_tok:
        ztok_ref[...] = _dot_nt(wtok_ref[...], h)
    else:
        z = _dot(h, wtok_ref[...])
        if norm_tok:
            z = _seg_rms64(z, tokg_ref[...]) * (DIFF_HEAD_DIM ** -0.5)
        ztok_ref[...] = z.astype(ztok_ref.dtype)
    z_mem = _dot(h, wmem_ref[...])
    omem_ref[...] = _mem_attention(z_mem, qg_ref[...], k_ref[0], v_ref[0]).astype(BF16)


def _in_proj(x, g, w_tok, w_mem, tok_gain, q_gain, k_mem, v_mem, *, seq, transposed_tok, tm=512):
    n_tok = x.shape[0]
    n_mem = k_mem.shape[1]
    tiles_per_seq = seq // tm
    if transposed_tok:
        tok_shape = jax.ShapeDtypeStruct((TOK_WIDTH, n_tok), F32)
        tok_spec = pl.BlockSpec((TOK_WIDTH, tm), lambda i: (0, i))
    else:
        tok_shape = jax.ShapeDtypeStruct((n_tok, TOK_WIDTH), BF16)
        tok_spec = pl.BlockSpec((tm, TOK_WIDTH), lambda i: (i, 0))
    kern = functools.partial(_in_proj_kernel, transposed_tok=transposed_tok,
                             norm_tok=not transposed_tok)
    return pl.pallas_call(
        kern,
        grid=(n_tok // tm,),
        in_specs=[
            pl.BlockSpec((tm, D_MODEL), lambda i: (i, 0)),
            _const_spec((1, D_MODEL)),
            _const_spec(w_tok.shape),
            _const_spec(w_mem.shape),
            _const_spec((1, TOK_WIDTH)),
            _const_spec((1, MEM_WIDTH)),
            pl.BlockSpec((1, n_mem, MEM_WIDTH), lambda i: (i // tiles_per_seq, 0, 0)),
            pl.BlockSpec((1, n_mem, MEM_WIDTH), lambda i: (i // tiles_per_seq, 0, 0)),
        ],
        out_specs=[tok_spec, pl.BlockSpec((tm, MEM_WIDTH), lambda i: (i, 0))],
        out_shape=[tok_shape, jax.ShapeDtypeStruct((n_tok, MEM_WIDTH), BF16)],
        compiler_params=_cparams(1),
        name="in_proj_t" if transposed_tok else "in_proj",
    )(x, g, w_tok, w_mem, tok_gain, q_gain, k_mem, v_mem)


def _cpow(ar, ai, dt, k):
    mag = jnp.exp(k * dt * ar)
    ang = k * dt * ai
    return mag * jnp.cos(ang), mag * jnp.sin(ang)


def _zoh_coef(ar, ai, dt):
    abr, abi = _cpow(ar, ai, dt, 1.0)
    den = ar * ar + ai * ai
    nr = abr - 1.0
    return (nr * ar + abi * ai) / den, (abi * ar - nr * ai) / den, abr, abi


def _ssm_prep_kernel(ldt_ref, arow_ref, acol_ref, bt_ref, ct_ref, m_ref, w_ref, v_ref, ac_ref):
    dt = jnp.exp(ldt_ref[0])
    first = lax.broadcasted_iota(jnp.int32, (1, LANES), 1) < SSM_N
    ar_r, ai_r = arow_ref[0, 0:1, :], arow_ref[0, 1:2, :]
    ar_c, ai_c = acol_ref[0, :, 0:1], acol_ref[0, :, 1:2]
    cf_re, cf_im, _, _ = _zoh_coef(ar_r, ai_r, dt)
    ab_re_c, ab_im_c = _cpow(ar_c, ai_c, dt, 1.0)

    pc_re, pc_im = _cpow(ar_r, ai_r, dt, float(SSM_C))
    ac_ref[0, 0:1, :] = pc_re
    ac_ref[0, 1:2, :] = jnp.where(first, -pc_im, pc_im)

    bre, bim = bt_ref[0, 0], bt_ref[0, 1]
    bb_re = cf_re * bre - cf_im * bim
    bb_im = cf_re * bim + cf_im * bre

    s_row = lax.broadcasted_iota(jnp.int32, (SSM_C, LANES), 0).astype(F32)
    e_re, e_im = _cpow(ar_r, ai_r, dt, float(SSM_C - 1) - s_row)
    for p in range(SSM_P):
        br, bi = bb_re[p:p + 1, :], bb_im[p:p + 1, :]
        w_ref[0, p * SSM_C:(p + 1) * SSM_C, :] = jnp.where(
            first, br * e_re - bi * e_im, br * e_im + bi * e_re).astype(BF16)

    lane_n = lax.broadcasted_iota(jnp.int32, (SSM_N, LANES), 1)
    tau = (lane_n % SSM_C).astype(F32)
    p0_re, p0_im = _cpow(ar_c, ai_c, dt, tau)
    p1_re = p0_re * ab_re_c - p0_im * ab_im_c
    p1_im = p0_re * ab_im_c + p0_im * ab_re_c
    reps = SSM_K // LANES
    p0_re, p0_im = jnp.concatenate([p0_re] * reps, 1), jnp.concatenate([p0_im] * reps, 1)
    p1_re, p1_im = jnp.concatenate([p1_re] * reps, 1), jnp.concatenate([p1_im] * reps, 1)

    def expand(ct):
        return jnp.concatenate(
            [jnp.where(lane_n < SSM_C, ct[:, 2 * j:2 * j + 1], ct[:, 2 * j + 1:2 * j + 2])
             for j in range(SSM_P // 2)], axis=1)

    c_re, c_im = expand(ct_ref[0, 0]), expand(ct_ref[0, 1])

    v_ref[0, 0:SSM_N, :] = (c_re * p1_re - c_im * p1_im).astype(BF16)
    v_ref[0, SSM_N:2 * SSM_N, :] = (-(c_re * p1_im + c_im * p1_re)).astype(BF16)

    ce = jnp.concatenate([c_re * p0_re - c_im * p0_im, -(c_re * p0_im + c_im * p0_re)], axis=0)
    kt = jnp.dot(jnp.where(first, bb_re, bb_im), ce, precision=lax.Precision.HIGHEST,
                 preferred_element_type=F32)

    row = lax.broadcasted_iota(jnp.int32, (SSM_C, SSM_K), 0)
    lane_t = lax.broadcasted_iota(jnp.int32, (SSM_C, SSM_K), 1) % SSM_C
    causal = lane_t >= row
    for p in range(SSM_P):
        base = jnp.broadcast_to(kt[p:p + 1, :], (SSM_C, SSM_K))
        shifted = pltpu.roll(base, 0, 1, stride=1, stride_axis=0)
        m_ref[0, p * SSM_C:(p + 1) * SSM_C, :] = jnp.where(causal, shifted, 0.0).astype(BF16)


def _ssm_prep(log_dt, a_row, a_col, b_t, c_t):
    g = log_dt.shape[0]
    return pl.pallas_call(
        _ssm_prep_kernel,
        grid=(g,),
        in_specs=[
            pl.BlockSpec((1, 1, 1), lambda i: (i, 0, 0)),
            pl.BlockSpec((1, 2, 2 * SSM_N), lambda i: (i, 0, 0)),
            pl.BlockSpec((1, SSM_N, 2), lambda i: (i, 0, 0)),
            pl.BlockSpec((1, 2, SSM_P, 2 * SSM_N), lambda i: (i, 0, 0, 0)),
            pl.BlockSpec((1, 2, SSM_N, SSM_P), lambda i: (i, 0, 0, 0)),
        ],
        out_specs=[
            pl.BlockSpec((1, SSM_K, SSM_K), lambda i: (i, 0, 0)),
            pl.BlockSpec((1, SSM_K, 2 * SSM_N), lambda i: (i, 0, 0)),
            pl.BlockSpec((1, 2 * SSM_N, SSM_K), lambda i: (i, 0, 0)),
            pl.BlockSpec((1, 2, 2 * SSM_N), lambda i: (i, 0, 0)),
        ],
        out_shape=[
            jax.ShapeDtypeStruct((g, SSM_K, SSM_K), BF16),
            jax.ShapeDtypeStruct((g, SSM_K, 2 * SSM_N), BF16),
            jax.ShapeDtypeStruct((g, 2 * SSM_N, SSM_K), BF16),
            jax.ShapeDtypeStruct((g, 2, 2 * SSM_N), F32),
        ],
        compiler_params=_cparams(1),
        name="ssm_prep",
    )(log_dt, a_row, a_col, b_t, c_t)


def _ssm_scan_kernel(u_ref, m_ref, w_ref, v_ref, ac_ref, d_ref, y_ref, z_scr, zs_scr, sp_scr,
                     *, bsz, n_chunks):
    u = u_ref[0]
    ub = u.astype(BF16)
    y = _dot(ub, m_ref[0])
    z = _dot(ub, w_ref[0])
    z_scr[...] = z
    zs_scr[...] = pltpu.roll(z, SSM_N, 1)

    a1 = ac_ref[0, 0:1, :]
    a2 = ac_ref[0, 1:2, :]
    s = jnp.zeros((bsz, 2 * SSM_N), F32)
    ss = jnp.zeros((bsz, 2 * SSM_N), F32)
    for c in range(n_chunks):
        rows = pl.ds(c, bsz, stride=n_chunks)
        sp_scr[rows, :] = s
        if c + 1 < n_chunks:
            s, ss = (a1 * s + a2 * ss + z_scr[rows, :],
                     a1 * ss - a2 * s + zs_scr[rows, :])
    y = y + _dot(sp_scr[...].astype(BF16), v_ref[0])
    y_ref[0] = y + d_ref[0] * u


def _ssm_scan(u_all, m, w, v, ac, d_rep, *, bsz):
    g, n_rows, _ = u_all.shape
    n_chunks = n_rows // bsz
    kern = functools.partial(_ssm_scan_kernel, bsz=bsz, n_chunks=n_chunks)
    return pl.pallas_call(
        kern,
        grid=(g,),
        in_specs=[
            pl.BlockSpec((1, n_rows, SSM_K), lambda i: (i, 0, 0)),
            pl.BlockSpec((1, SSM_K, SSM_K), lambda i: (i, 0, 0)),
            pl.BlockSpec((1, SSM_K, 2 * SSM_N), lambda i: (i, 0, 0)),
            pl.BlockSpec((1, 2 * SSM_N, SSM_K), lambda i: (i, 0, 0)),
            pl.BlockSpec((1, 2, 2 * SSM_N), lambda i: (i, 0, 0)),
            pl.BlockSpec((1, 1, SSM_K), lambda i: (i, 0, 0)),
        ],
        out_specs=pl.BlockSpec((1, n_rows, SSM_K), lambda i: (i, 0, 0)),
        out_shape=jax.ShapeDtypeStruct((g, n_rows, SSM_K), F32),
        scratch_shapes=[pltpu.VMEM((n_rows, 2 * SSM_N), F32)] * 3,
        compiler_params=_cparams(1),
        name="ssm_scan",
    )(u_all, m, w, v, ac, d_rep)


def _shared_kv_kernel(x_ref, g_ref, wk_ref, wv_ref, kg_ref, k_ref, v_ref):
    h = _rms(x_ref[...], g_ref[...]).astype(BF16)
    k_ref[...] = _seg_rms64(_dot(h, wk_ref[...]), kg_ref[...]).astype(BF16)
    v_ref[...] = _dot(h, wv_ref[...]).astype(BF16)


def _shared_kv(x, g, wk, wv, k_gain, tm=512):
    n_tok = x.shape[0]
    out = jax.ShapeDtypeStruct((n_tok, TOK_WIDTH), BF16)
    return pl.pallas_call(
        _shared_kv_kernel,
        grid=(n_tok // tm,),
        in_specs=[
            pl.BlockSpec((tm, D_MODEL), lambda i: (i, 0)),
            _const_spec((1, D_MODEL)),
            _const_spec(wk.shape),
            _const_spec(wv.shape),
            _const_spec((1, TOK_WIDTH)),
        ],
        out_specs=[pl.BlockSpec((tm, TOK_WIDTH), lambda i: (i, 0))] * 2,
        out_shape=[out, out],
        compiler_params=_cparams(1),
        name="shared_kv",
    )(x, g, wk, wv, k_gain)


def _diff_attn_kernel(q_ref, k_ref, v_ref, lam_ref, subg_ref, o_ref, m_scr, l_scr, acc_scr,
                      *, tq, lambda_init):
    qi = pl.program_id(1)
    lp = lam_ref[...]
    lam = (jnp.exp(jnp.sum(lp[0:1] * lp[1:2], axis=-1, keepdims=True))
           - jnp.exp(jnp.sum(lp[2:3] * lp[3:4], axis=-1, keepdims=True)) + lambda_init)
    lo, hi = _half_masks(tq, BF16)
    row_chunk = lax.broadcasted_iota(jnp.int32, (tq, tq), 0) // CHUNK
    col_chunk = lax.broadcasted_iota(jnp.int32, (tq, tq), 1) // CHUNK
    diag_mask = col_chunk <= row_chunk
    n_grp = DIFF_HEADS // 2

    for head in range(DIFF_HEADS):
        grp, half = head // 2, head % 2
        hmask = hi if half else lo
        vsl = slice(head * DIFF_V_DIM, (head + 1) * DIFF_V_DIM)
        comps = []
        for comp in range(2):
            ksl = slice((comp * n_grp + grp) * LANES, (comp * n_grp + grp + 1) * LANES)
            q = q_ref[:, ksl] * hmask
            m_scr[...] = jnp.full((tq, 1), -jnp.inf, F32)
            l_scr[...] = jnp.zeros((tq, 1), F32)
            acc_scr[...] = jnp.zeros((tq, DIFF_V_DIM), F32)

            def block(j, masked, q=q, ksl=ksl, vsl=vsl):
                rows = pl.ds(pl.multiple_of(j * tq, tq), tq)
                s = _dot_nt(q, k_ref[rows, ksl])
                if masked:
                    s = jnp.where(diag_mask, s, -jnp.inf)
                m_old = m_scr[...]
                m_new = jnp.maximum(m_old, jnp.max(s, axis=-1, keepdims=True))
                alpha = jnp.exp(m_old - m_new)
                p = jnp.exp(s - m_new)
                l_scr[...] = alpha * l_scr[...] + jnp.sum(p, axis=-1, keepdims=True)
                acc_scr[...] = alpha * acc_scr[...] + _dot(p.astype(BF16), v_ref[rows, vsl])
                m_scr[...] = m_new

            def body(j, carry):
                block(j, False)
                return carry

            lax.fori_loop(0, qi, body, 0)
            block(qi, True)
            comps.append(acc_scr[...] / l_scr[...])
        o = comps[0] - lam * comps[1]
        o = o * lax.rsqrt(jnp.mean(o * o, axis=-1, keepdims=True) + EPS)
        o_ref[:, vsl] = (o * subg_ref[:, vsl] * (1.0 - lambda_init)).astype(o_ref.dtype)


def _diff_attn(q, k, v, lam_params, sub_gain, *, bsz, seq, lambda_init, tq=256):
    n_q = seq // tq
    kern = functools.partial(_diff_attn_kernel, tq=tq, lambda_init=lambda_init)
    return pl.pallas_call(
        kern,
        grid=(bsz, n_q),
        in_specs=[
            pl.BlockSpec((tq, TOK_WIDTH), lambda b, i: (b * n_q + i, 0)),
            pl.BlockSpec((seq, TOK_WIDTH), lambda b, i: (b, 0)),
            pl.BlockSpec((seq, TOK_WIDTH), lambda b, i: (b, 0)),
            pl.BlockSpec((4, DIFF_HEAD_DIM), lambda b, i: (0, 0)),
            pl.BlockSpec((1, TOK_WIDTH), lambda b, i: (0, 0)),
        ],
        out_specs=pl.BlockSpec((tq, TOK_WIDTH), lambda b, i: (b * n_q + i, 0)),
        out_shape=jax.ShapeDtypeStruct((bsz * seq, TOK_WIDTH), BF16),
        scratch_shapes=[pltpu.VMEM((tq, 1), F32), pltpu.VMEM((tq, 1), F32),
                        pltpu.VMEM((tq, DIFF_V_DIM), F32)],
        compiler_params=_cparams(2),
        name="diff_attn",
    )(q, k, v, lam_params, sub_gain)


def _gelu_tanh(x):
    return 0.5 * x * (1.0 + jnp.tanh(math.sqrt(2.0 / math.pi) * (x + 0.044715 * (x * x * x))))


def _sigmoid(x):
    return 1.0 / (1.0 + jnp.exp(-x))


def _post_kernel(tok_ref, omem_ref, x_ref, wglu_ref, wot_ref, wom_ref, g_ref, wg_ref, wu_ref,
                 wd_ref, o_ref, *, s5):
    if s5:
        h = _gelu_tanh(tok_ref[...].T)
        gate = _sigmoid(_dot(h.astype(BF16), wglu_ref[...]))
        o_tok = (h * gate).astype(BF16)
    else:
        o_tok = tok_ref[...]
    x1 = x_ref[...] + _dot(o_tok, wot_ref[...]) + _dot(omem_ref[...], wom_ref[...])
    h2 = _rms(x1, g_ref[...]).astype(BF16)
    gate = _dot(h2, wg_ref[...])
    up = _dot(h2, wu_ref[...])
    act = (gate * _sigmoid(gate) * up).astype(BF16)
    o_ref[...] = x1 + _dot(act, wd_ref[...])


def _post(tok, o_mem, x, w_glu, w_out_tok, w_out_mem, g, wg, wu, wd, *, s5, tm=256):
    n_tok = x.shape[0]
    if s5:
        tok_spec = pl.BlockSpec((TOK_WIDTH, tm), lambda i: (0, i))
    else:
        tok_spec = pl.BlockSpec((tm, TOK_WIDTH), lambda i: (i, 0))
    return pl.pallas_call(
        functools.partial(_post_kernel, s5=s5),
        grid=(n_tok // tm,),
        in_specs=[
            tok_spec,
            pl.BlockSpec((tm, MEM_WIDTH), lambda i: (i, 0)),
            pl.BlockSpec((tm, D_MODEL), lambda i: (i, 0)),
            _const_spec(w_glu.shape),
            _const_spec(w_out_tok.shape),
            _const_spec(w_out_mem.shape),
            _const_spec((1, D_MODEL)),
            _const_spec(wg.shape),
            _const_spec(wu.shape),
            _const_spec(wd.shape),
        ],
        out_specs=pl.BlockSpec((tm, D_MODEL), lambda i: (i, 0)),
        out_shape=jax.ShapeDtypeStruct((n_tok, D_MODEL), F32),
        compiler_params=_cparams(1),
        name="post_s5" if s5 else "post",
    )(tok, o_mem, x, w_glu, w_out_tok, w_out_mem, g, wg, wu, wd)


def _tile_gain(g, reps):
    return jnp.tile(g.astype(F32), reps)[None, :]


def kernel(x, mem, norm_mix, norm_ffn, norm_mem, w_in, w_out, mem_wk, mem_wv, mem_q_norm,
           mem_k_norm, ffn_w_gate, ffn_w_up, ffn_w_down, ssm_a_re, ssm_a_im, ssm_log_dt,
           ssm_b_re, ssm_b_im, ssm_c_re, ssm_c_im, ssm_d, ssm_w_glu, kv_norm, diff_wk, diff_wv,
           diff_k_norm, diff_q_norm, diff_lambda_q1, diff_lambda_k1, diff_lambda_q2,
           diff_lambda_k2, diff_sub_norm):
    bsz, seq, _ = x.shape
    n_tok = bsz * seq
    n_chunks = seq // SSM_C
    xf = x.reshape(n_tok, D_MODEL)

    wkv = jnp.concatenate([mem_wk, mem_wv], axis=-1).astype(BF16)
    k_gain = jnp.tile(mem_k_norm.astype(F32), (1, MEM_HEADS))[:, None, :]
    k_mem, v_mem = _mem_kv(mem, norm_mem[:, None, :], wkv, k_gain)

    k_sh = v_sh = None
    for layer in range(DEPTH):
        s5 = layer < N_A_LAYERS
        w_in_l = w_in[layer].astype(BF16)
        w_tok = w_in_l[:, :TOK_WIDTH]
        q_gain = _tile_gain(mem_q_norm[layer], MEM_HEADS)
        if s5:
            tok_gain = jnp.ones((1, TOK_WIDTH), F32)
            w_tok = w_tok.T
        else:
            tok_gain = _tile_gain(diff_q_norm[layer - N_A_LAYERS], 2 * DIFF_HEADS)
        z_tok, o_mem = _in_proj(xf, norm_mix[layer][None, :], w_tok, w_in_l[:, TOK_WIDTH:],
                                tok_gain, q_gain, k_mem[layer], v_mem[layer],
                                seq=seq, transposed_tok=s5)
        if s5:
            i = layer
            log_dt = ssm_log_dt[i][:, None, None]
            a_row = jnp.tile(jnp.stack([ssm_a_re[i], ssm_a_im[i]], axis=1), (1, 1, 2))
            a_col = jnp.stack([ssm_a_re[i], ssm_a_im[i]], axis=2)
            b_t = jnp.tile(jnp.stack([ssm_b_re[i], ssm_b_im[i]], axis=1).transpose(0, 1, 3, 2),
                           (1, 1, 1, 2))
            c_t = jnp.stack([ssm_c_re[i], ssm_c_im[i]], axis=1).transpose(0, 1, 3, 2)
            m_op, w_op, v_op, ac = _ssm_prep(log_dt, a_row, a_col, b_t, c_t)
            d_rep = jnp.repeat(ssm_d[i].reshape(SSM_G, 1, SSM_P), SSM_C, axis=2)
            u_all = (z_tok.reshape(SSM_G, SSM_P, bsz * n_chunks, SSM_C)
                     .transpose(0, 2, 1, 3).reshape(SSM_G, bsz * n_chunks, SSM_K))
            y_all = _ssm_scan(u_all, m_op, w_op, v_op, ac, d_rep, bsz=bsz)
            tok = (y_all.reshape(SSM_G, bsz * n_chunks, SSM_P, SSM_C)
                   .transpose(0, 2, 1, 3).reshape(TOK_WIDTH, n_tok))
            w_glu = ssm_w_glu[i].astype(BF16)
        else:
            j = layer - N_A_LAYERS
            lambda_init = 0.8 - 0.6 * math.exp(-0.3 * layer)
            lam_params = jnp.stack([diff_lambda_q1[j], diff_lambda_k1[j],
                                    diff_lambda_q2[j], diff_lambda_k2[j]]).astype(F32)
            tok = _diff_attn(z_tok, k_sh, v_sh, lam_params,
                             _tile_gain(diff_sub_norm[j], DIFF_HEADS),
                             bsz=bsz, seq=seq, lambda_init=lambda_init)
            w_glu = jnp.zeros((8, LANES), BF16)
        w_out_l = w_out[layer].astype(BF16)
        xf = _post(tok, o_mem, xf, w_glu, w_out_l[:TOK_WIDTH], w_out_l[TOK_WIDTH:],
                   norm_ffn[layer][None, :], ffn_w_gate[layer].astype(BF16),
                   ffn_w_up[layer].astype(BF16), ffn_w_down[layer].astype(BF16), s5=s5)
        if layer == N_A_LAYERS - 1:
            k_sh, v_sh = _shared_kv(xf, kv_norm[None, :], diff_wk.astype(BF16),
                                    diff_wv.astype(BF16),
                                    _tile_gain(diff_k_norm, 2 * DIFF_HEADS))
    return xf.reshape(bsz, seq, D_MODEL)
```

```python
import functools
import math

import jax
import jax.numpy as jnp
from jax import lax
from jax.experimental import pallas as pl
from jax.experimental.pallas import tpu as pltpu

D_MODEL = 1024
DEPTH = 4
CHUNK = 64
N_A_LAYERS = DEPTH // 2
MEM_HEADS = 4
MEM_HEAD_DIM = 64
MEM_WIDTH = MEM_HEADS * MEM_HEAD_DIM
TOK_WIDTH = D_MODEL - MEM_WIDTH
SSM_P = 16
SSM_G = TOK_WIDTH // SSM_P
SSM_N = 64
SSM_C = 64
SSM_K = SSM_P * SSM_C
DIFF_HEAD_DIM = 64
DIFF_HEADS = TOK_WIDTH // (2 * DIFF_HEAD_DIM)
DIFF_V_DIM = 2 * DIFF_HEAD_DIM
D_FF = 2816
EPS = 1e-6

LANES = 128
VMEM_LIMIT = 56 * 1024 * 1024

F32 = jnp.float32
BF16 = jnp.bfloat16


def _cparams(n_grid, vmem=VMEM_LIMIT):
    return pltpu.CompilerParams(
        dimension_semantics=("arbitrary",) * n_grid, vmem_limit_bytes=vmem)


def _const_spec(shape):
    nd = len(shape)
    return pl.BlockSpec(shape, lambda *_: (0,) * nd, pipeline_mode=pl.Buffered(1))


def _rms(x, g):
    return x * lax.rsqrt(jnp.mean(x * x, axis=-1, keepdims=True) + EPS) * g


def _seg_rms64(x, g):
    rows, width = x.shape
    lane = lax.broadcasted_iota(jnp.int32, (rows, LANES), 1)
    lo = lane < 64
    outs = []
    for v in range(width // LANES):
        xv = x[:, v * LANES:(v + 1) * LANES]
        sq = xv * xv
        s_lo = jnp.sum(jnp.where(lo, sq, 0.0), axis=-1, keepdims=True)
        s_hi = jnp.sum(jnp.where(lo, 0.0, sq), axis=-1, keepdims=True)
        r_lo = lax.rsqrt(s_lo * (1.0 / 64) + EPS)
        r_hi = lax.rsqrt(s_hi * (1.0 / 64) + EPS)
        outs.append(xv * jnp.where(lo, r_lo, r_hi) * g[:, v * LANES:(v + 1) * LANES])
    return jnp.concatenate(outs, axis=1)


def _half_masks(rows, dtype):
    lane = lax.broadcasted_iota(jnp.int32, (rows, LANES), 1)
    lo = (lane < 64).astype(dtype)
    return lo, (1 - lo).astype(dtype)


def _dot(a, b):
    return jnp.dot(a, b, preferred_element_type=F32)


def _dot_nt(a, b):
    return lax.dot_general(a, b, (((1,), (1,)), ((), ())), preferred_element_type=F32)


def _mem_kv_kernel(mem_ref, g_ref, wkv_ref, kg_ref, k_ref, v_ref):
    m = mem_ref[0]
    mn = _rms(m, g_ref[0]).astype(BF16)
    kv = _dot(mn, wkv_ref[0])
    k = _seg_rms64(kv[:, :MEM_WIDTH], kg_ref[0])
    k_ref[0, 0] = k.astype(BF16)
    v_ref[0, 0] = kv[:, MEM_WIDTH:].astype(BF16)


def _mem_kv(mem, norm_mem, wkv, k_gain):
    bsz, n_mem, _ = mem.shape
    out = jax.ShapeDtypeStruct((DEPTH, bsz, n_mem, MEM_WIDTH), BF16)
    return pl.pallas_call(
        _mem_kv_kernel,
        grid=(DEPTH, bsz),
        in_specs=[
            pl.BlockSpec((1, n_mem, D_MODEL), lambda l, b: (b, 0, 0)),
            pl.BlockSpec((1, 1, D_MODEL), lambda l, b: (l, 0, 0)),
            pl.BlockSpec((1, D_MODEL, 2 * MEM_WIDTH), lambda l, b: (l, 0, 0)),
            pl.BlockSpec((1, 1, MEM_WIDTH), lambda l, b: (l, 0, 0)),
        ],
        out_specs=[pl.BlockSpec((1, 1, n_mem, MEM_WIDTH), lambda l, b: (l, b, 0, 0))] * 2,
        out_shape=[out, out],
        compiler_params=_cparams(2),
        name="mem_kv",
    )(mem, norm_mem, wkv, k_gain)


def _mem_attention(z_mem, qg, k, v):
    rows = z_mem.shape[0]
    q = (_seg_rms64(z_mem, qg) * (MEM_HEAD_DIM ** -0.5)).astype(BF16)
    lo_q, hi_q = _half_masks(rows, BF16)
    lo_v, hi_v = _half_masks(v.shape[0], BF16)
    outs = []
    for grp in range(MEM_WIDTH // LANES):
        sl = slice(grp * LANES, (grp + 1) * LANES)
        o = jnp.zeros((rows, LANES), F32)
        for mq, mv in ((lo_q, lo_v), (hi_q, hi_v)):
            s = _dot_nt(q[:, sl] * mq, k[:, sl])
            s = s - jnp.max(s, axis=-1, keepdims=True)
            p = jnp.exp(s)
            p = p / jnp.sum(p, axis=-1, keepdims=True)
            o = o + _dot(p.astype(BF16), v[:, sl] * mv)
        outs.append(o)
    return jnp.concatenate(outs, axis=1)


def _seg_rms_rows(xt, g_col, seg):
    width, cols = xt.shape
    x3 = xt.reshape(width // seg, seg, cols)
    ms = jnp.mean(x3 * x3, axis=1, keepdims=True)
    return (x3 * lax.rsqrt(ms + EPS)).reshape(width, cols) * g_col


def _in_proj_kernel(x_ref, g_ref, wtok_ref, wmem_ref, tokg_ref, qg_ref, k_ref, v_ref,
                    ztok_ref, omem_ref, *, qk_norm):
    h = _rms(x_ref[...], g_ref[...]).astype(BF16)
    zt = _dot_nt(wtok_ref[...], h)
    if qk_norm:
        zt = _seg_rms_rows(zt, tokg_ref[...], DIFF_HEAD_DIM) * (DIFF_HEAD_DIM ** -0.5)
    ztok_ref[...] = zt.astype(ztok_ref.dtype)
    z_mem = _dot(h, wmem_ref[...])
    omem_ref[...] = _mem_attention(z_mem, qg_ref[...], k_ref[0], v_ref[0]).astype(BF16)


def _in_proj(x, g, w_tok_t, w_mem, tok_gain, q_gain, k_mem, v_mem, *, seq, qk_norm, tm=512):
    n_tok = x.shape[0]
    n_mem = k_mem.shape[1]
    tiles_per_seq = seq // tm
    return pl.pallas_call(
        functools.partial(_in_proj_kernel, qk_norm=qk_norm),
        grid=(n_tok // tm,),
        in_specs=[
            pl.BlockSpec((tm, D_MODEL), lambda i: (i, 0)),
            _const_spec((1, D_MODEL)),
            _const_spec(w_tok_t.shape),
            _const_spec(w_mem.shape),
            _const_spec((TOK_WIDTH, 1)),
            _const_spec((1, MEM_WIDTH)),
            pl.BlockSpec((1, n_mem, MEM_WIDTH), lambda i: (i // tiles_per_seq, 0, 0)),
            pl.BlockSpec((1, n_mem, MEM_WIDTH), lambda i: (i // tiles_per_seq, 0, 0)),
        ],
        out_specs=[pl.BlockSpec((TOK_WIDTH, tm), lambda i: (0, i)),
                   pl.BlockSpec((tm, MEM_WIDTH), lambda i: (i, 0))],
        out_shape=[jax.ShapeDtypeStruct((TOK_WIDTH, n_tok), BF16 if qk_norm else F32),
                   jax.ShapeDtypeStruct((n_tok, MEM_WIDTH), BF16)],
        compiler_params=_cparams(1),
        name="in_proj_q" if qk_norm else "in_proj_u",
    )(x, g, w_tok_t, w_mem, tok_gain, q_gain, k_mem, v_mem)


def _cpow(ar, ai, dt, k):
    mag = jnp.exp(k * dt * ar)
    ang = k * dt * ai
    return mag * jnp.cos(ang), mag * jnp.sin(ang)


def _zoh_coef(ar, ai, dt):
    abr, abi = _cpow(ar, ai, dt, 1.0)
    den = ar * ar + ai * ai
    nr = abr - 1.0
    return (nr * ar + abi * ai) / den, (abi * ar - nr * ai) / den, abr, abi


def _ssm_prep_kernel(ldt_ref, arow_ref, acol_ref, bt_ref, ct_ref, m_ref, w_ref, v_ref, ac_ref):
    dt = jnp.exp(ldt_ref[0])
    first = lax.broadcasted_iota(jnp.int32, (1, LANES), 1) < SSM_N
    ar_r, ai_r = arow_ref[0, 0:1, :], arow_ref[0, 1:2, :]
    ar_c, ai_c = acol_ref[0, :, 0:1], acol_ref[0, :, 1:2]
    cf_re, cf_im, _, _ = _zoh_coef(ar_r, ai_r, dt)
    ab_re_c, ab_im_c = _cpow(ar_c, ai_c, dt, 1.0)

    pc_re, pc_im = _cpow(ar_r, ai_r, dt, float(SSM_C))
    ac_ref[0, 0:1, :] = pc_re
    ac_ref[0, 1:2, :] = jnp.where(first, -pc_im, pc_im)

    bre, bim = bt_ref[0, 0], bt_ref[0, 1]
    bb_re = cf_re * bre - cf_im * bim
    bb_im = cf_re * bim + cf_im * bre

    s_row = lax.broadcasted_iota(jnp.int32, (SSM_C, LANES), 0).astype(F32)
    e_re, e_im = _cpow(ar_r, ai_r, dt, float(SSM_C - 1) - s_row)
    for p in range(SSM_P):
        br, bi = bb_re[p:p + 1, :], bb_im[p:p + 1, :]
        w_ref[0, p * SSM_C:(p + 1) * SSM_C, :] = jnp.where(
            first, br * e_re - bi * e_im, br * e_im + bi * e_re).astype(BF16)

    lane_n = lax.broadcasted_iota(jnp.int32, (SSM_N, LANES), 1)
    tau = (lane_n % SSM_C).astype(F32)
    p0_re, p0_im = _cpow(ar_c, ai_c, dt, tau)
    p1_re = p0_re * ab_re_c - p0_im * ab_im_c
    p1_im = p0_re * ab_im_c + p0_im * ab_re_c
    reps = SSM_K // LANES
    p0_re, p0_im = jnp.concatenate([p0_re] * reps, 1), jnp.concatenate([p0_im] * reps, 1)
    p1_re, p1_im = jnp.concatenate([p1_re] * reps, 1), jnp.concatenate([p1_im] * reps, 1)

    def expand(ct):
        return jnp.concatenate(
            [jnp.where(lane_n < SSM_C, ct[:, 2 * j:2 * j + 1], ct[:, 2 * j + 1:2 * j + 2])
             for j in range(SSM_P // 2)], axis=1)

    c_re, c_im = expand(ct_ref[0, 0]), expand(ct_ref[0, 1])

    v_ref[0, 0:SSM_N, :] = (c_re * p1_re - c_im * p1_im).astype(BF16)
    v_ref[0, SSM_N:2 * SSM_N, :] = (-(c_re * p1_im + c_im * p1_re)).astype(BF16)

    ce = jnp.concatenate([c_re * p0_re - c_im * p0_im, -(c_re * p0_im + c_im * p0_re)], axis=0)
    kt = jnp.dot(jnp.where(first, bb_re, bb_im), ce, precision=lax.Precision.HIGHEST,
                 preferred_element_type=F32)

    row = lax.broadcasted_iota(jnp.int32, (SSM_C, SSM_K), 0)
    lane_t = lax.broadcasted_iota(jnp.int32, (SSM_C, SSM_K), 1) % SSM_C
    causal = lane_t >= row
    for p in range(SSM_P):
        base = jnp.broadcast_to(kt[p:p + 1, :], (SSM_C, SSM_K))
        shifted = pltpu.roll(base, 0, 1, stride=1, stride_axis=0)
        m_ref[0, p * SSM_C:(p + 1) * SSM_C, :] = jnp.where(causal, shifted, 0.0).astype(BF16)


def _ssm_prep(log_dt, a_row, a_col, b_t, c_t):
    g = log_dt.shape[0]
    return pl.pallas_call(
        _ssm_prep_kernel,
        grid=(g,),
        in_specs=[
            pl.BlockSpec((1, 1, 1), lambda i: (i, 0, 0)),
            pl.BlockSpec((1, 2, 2 * SSM_N), lambda i: (i, 0, 0)),
            pl.BlockSpec((1, SSM_N, 2), lambda i: (i, 0, 0)),
            pl.BlockSpec((1, 2, SSM_P, 2 * SSM_N), lambda i: (i, 0, 0, 0)),
            pl.BlockSpec((1, 2, SSM_N, SSM_P), lambda i: (i, 0, 0, 0)),
        ],
        out_specs=[
            pl.BlockSpec((1, SSM_K, SSM_K), lambda i: (i, 0, 0)),
            pl.BlockSpec((1, SSM_K, 2 * SSM_N), lambda i: (i, 0, 0)),
            pl.BlockSpec((1, 2 * SSM_N, SSM_K), lambda i: (i, 0, 0)),
            pl.BlockSpec((1, 2, 2 * SSM_N), lambda i: (i, 0, 0)),
        ],
        out_shape=[
            jax.ShapeDtypeStruct((g, SSM_K, SSM_K), BF16),
            jax.ShapeDtypeStruct((g, SSM_K, 2 * SSM_N), BF16),
            jax.ShapeDtypeStruct((g, 2 * SSM_N, SSM_K), BF16),
            jax.ShapeDtypeStruct((g, 2, 2 * SSM_N), F32),
        ],
        compiler_params=_cparams(1),
        name="ssm_prep",
    )(log_dt, a_row, a_col, b_t, c_t)


def _ssm_scan_kernel(u_ref, m_ref, w_ref, v_ref, ac_ref, d_ref, y_ref, z_scr, zs_scr, sp_scr,
                     *, bsz, n_chunks):
    u = u_ref[0]
    ub = u.astype(BF16)
    y = _dot(ub, m_ref[0])
    z = _dot(ub, w_ref[0])
    z_scr[...] = z
    zs_scr[...] = pltpu.roll(z, SSM_N, 1)

    a1 = ac_ref[0, 0:1, :]
    a2 = ac_ref[0, 1:2, :]
    s = jnp.zeros((bsz, 2 * SSM_N), F32)
    ss = jnp.zeros((bsz, 2 * SSM_N), F32)
    for c in range(n_chunks):
        rows = pl.ds(c, bsz, stride=n_chunks)
        sp_scr[rows, :] = s
        if c + 1 < n_chunks:
            s, ss = (a1 * s + a2 * ss + z_scr[rows, :],
                     a1 * ss - a2 * s + zs_scr[rows, :])
    y = y + _dot(sp_scr[...].astype(BF16), v_ref[0])
    y_ref[0] = y + d_ref[0] * u


def _ssm_scan(u_all, m, w, v, ac, d_rep, *, bsz):
    g, n_rows, _ = u_all.shape
    n_chunks = n_rows // bsz
    kern = functools.partial(_ssm_scan_kernel, bsz=bsz, n_chunks=n_chunks)
    return pl.pallas_call(
        kern,
        grid=(g,),
        in_specs=[
            pl.BlockSpec((1, n_rows, SSM_K), lambda i: (i, 0, 0)),
            pl.BlockSpec((1, SSM_K, SSM_K), lambda i: (i, 0, 0)),
            pl.BlockSpec((1, SSM_K, 2 * SSM_N), lambda i: (i, 0, 0)),
            pl.BlockSpec((1, 2 * SSM_N, SSM_K), lambda i: (i, 0, 0)),
            pl.BlockSpec((1, 2, 2 * SSM_N), lambda i: (i, 0, 0)),
            pl.BlockSpec((1, 1, SSM_K), lambda i: (i, 0, 0)),
        ],
        out_specs=pl.BlockSpec((1, n_rows, SSM_K), lambda i: (i, 0, 0)),
        out_shape=jax.ShapeDtypeStruct((g, n_rows, SSM_K), F32),
        scratch_shapes=[pltpu.VMEM((n_rows, 2 * SSM_N), F32)] * 3,
        compiler_params=_cparams(1),
        name="ssm_scan",
    )(u_all, m, w, v, ac, d_rep)


ATTN_BLOCK = 256


def _shared_kv_kernel(x_ref, g_ref, wk_ref, wvt_ref, kg_ref, k_ref, vt_ref):
    h = _rms(x_ref[...], g_ref[...]).astype(BF16)
    k_ref[...] = _seg_rms64(_dot(h, wk_ref[...]), kg_ref[...]).astype(BF16)
    vt_ref[0] = _dot_nt(wvt_ref[...], h).astype(BF16)


def _shared_kv(x, g, wk, wv_t, k_gain):
    n_tok = x.shape[0]
    tm = ATTN_BLOCK
    return pl.pallas_call(
        _shared_kv_kernel,
        grid=(n_tok // tm,),
        in_specs=[
            pl.BlockSpec((tm, D_MODEL), lambda i: (i, 0)),
            _const_spec((1, D_MODEL)),
            _const_spec(wk.shape),
            _const_spec(wv_t.shape),
            _const_spec((1, TOK_WIDTH)),
        ],
        out_specs=[pl.BlockSpec((tm, TOK_WIDTH), lambda i: (i, 0)),
                   pl.BlockSpec((1, TOK_WIDTH, tm), lambda i: (i, 0, 0))],
        out_shape=[jax.ShapeDtypeStruct((n_tok, TOK_WIDTH), BF16),
                   jax.ShapeDtypeStruct((n_tok // tm, TOK_WIDTH, tm), BF16)],
        compiler_params=_cparams(1),
        name="shared_kv",
    )(x, g, wk, wv_t, k_gain)


N_STREAMS = 2 * DIFF_HEADS


def _diff_attn_kernel(qt_ref, k_ref, vt_ref, lam_ref, subg_ref, o_ref, qm_scr, sc_scr, p_scr,
                      bm_scr, m_scr, l_scr, al_scr, acc_scr, *, lambda_init):
    tq = ATTN_BLOCK
    qi = pl.program_id(1)
    lp = lam_ref[...]
    lam = (jnp.exp(jnp.sum(lp[0:1] * lp[1:2], axis=-1, keepdims=True))
           - jnp.exp(jnp.sum(lp[2:3] * lp[3:4], axis=-1, keepdims=True)) + lambda_init)
    key_chunk = lax.broadcasted_iota(jnp.int32, (tq, tq), 0) // CHUNK
    qry_chunk = lax.broadcasted_iota(jnp.int32, (tq, tq), 1) // CHUNK
    diag_mask = key_chunk <= qry_chunk
    first_half = lax.broadcasted_iota(jnp.int32, (LANES, tq), 0) < DIFF_HEAD_DIM
    n_grp = DIFF_HEADS // 2

    for kg in range(2 * n_grp):
        qg = qt_ref[kg * LANES:(kg + 1) * LANES, :]
        zero = jnp.zeros_like(qg)
        qm_scr[kg, :, 0:tq] = jnp.where(first_half, qg, zero)
        qm_scr[kg, :, tq:2 * tq] = jnp.where(first_half, zero, qg)
    m_scr[...] = jnp.full(m_scr.shape, -jnp.inf, F32)
    l_scr[...] = jnp.zeros(l_scr.shape, F32)
    acc_scr[...] = jnp.zeros(acc_scr.shape, F32)

    def block(j, masked):
        rows = pl.ds(pl.multiple_of(j * tq, tq), tq)
        for kg in range(2 * n_grp):
            kb = k_ref[rows, kg * LANES:(kg + 1) * LANES]
            st = _dot(kb, qm_scr[kg])
            if masked:
                st = jnp.where(jnp.concatenate([diag_mask, diag_mask], axis=1), st, -jnp.inf)
            sc_scr[kg] = st
            bm_scr[kg] = jnp.max(st, axis=0, keepdims=True)
        for kg in range(2 * n_grp):
            comp, g = divmod(kg, n_grp)
            m_old = m_scr[kg]
            m_new = jnp.maximum(m_old, bm_scr[kg])
            alpha = jnp.exp(m_old - m_new)
            p = jnp.exp(sc_scr[kg] - m_new)
            l_scr[kg] = alpha * l_scr[kg] + jnp.sum(p, axis=0, keepdims=True)
            m_scr[kg] = m_new
            pb = p.astype(BF16)
            for hh in range(2):
                c_cols = slice(comp * tq, (comp + 1) * tq)
                al_scr[2 * g + hh, :, c_cols] = alpha[:, hh * tq:(hh + 1) * tq]
                p_scr[2 * g + hh, :, c_cols] = pb[:, hh * tq:(hh + 1) * tq]
        for head in range(DIFF_HEADS):
            vt = vt_ref[j, head * DIFF_V_DIM:(head + 1) * DIFF_V_DIM, :]
            acc_scr[head] = al_scr[head] * acc_scr[head] + _dot(vt, p_scr[head])

    def body(j, carry):
        block(j, False)
        return carry

    lax.fori_loop(0, qi, body, 0)
    block(qi, True)

    for head in range(DIFF_HEADS):
        rows = slice(head * DIFF_V_DIM, (head + 1) * DIFF_V_DIM)
        q_cols = slice((head % 2) * tq, (head % 2 + 1) * tq)
        l_pair = jnp.concatenate([l_scr[head // 2][:, q_cols],
                                  l_scr[n_grp + head // 2][:, q_cols]], axis=1)
        on = acc_scr[head] / l_pair
        o = on[:, 0:tq] - lam * on[:, tq:2 * tq]
        o = o * lax.rsqrt(jnp.mean(o * o, axis=0, keepdims=True) + EPS)
        o_ref[rows, :] = (o * subg_ref[rows, :] * (1.0 - lambda_init)).astype(o_ref.dtype)


def _diff_attn(qt, k, vt, lam_params, sub_gain, *, bsz, seq, lambda_init):
    tq = ATTN_BLOCK
    n_q = seq // tq
    return pl.pallas_call(
        functools.partial(_diff_attn_kernel, lambda_init=lambda_init),
        grid=(bsz, n_q),
        in_specs=[
            pl.BlockSpec((TOK_WIDTH, tq), lambda b, i: (0, b * n_q + i)),
            pl.BlockSpec((seq, TOK_WIDTH), lambda b, i: (b, 0)),
            pl.BlockSpec((n_q, TOK_WIDTH, tq), lambda b, i: (b, 0, 0)),
            pl.BlockSpec((4, DIFF_HEAD_DIM), lambda b, i: (0, 0)),
            pl.BlockSpec((TOK_WIDTH, 1), lambda b, i: (0, 0)),
        ],
        out_specs=pl.BlockSpec((TOK_WIDTH, tq), lambda b, i: (0, b * n_q + i)),
        out_shape=jax.ShapeDtypeStruct((TOK_WIDTH, bsz * seq), F32),
        scratch_shapes=[pltpu.VMEM((DIFF_HEADS, LANES, 2 * tq), BF16),
                        pltpu.VMEM((DIFF_HEADS, tq, 2 * tq), F32),
                        pltpu.VMEM((DIFF_HEADS, tq, 2 * tq), BF16),
                        pltpu.VMEM((DIFF_HEADS, 1, 2 * tq), F32),
                        pltpu.VMEM((DIFF_HEADS, 1, 2 * tq), F32),
                        pltpu.VMEM((DIFF_HEADS, 1, 2 * tq), F32),
                        pltpu.VMEM((DIFF_HEADS, 1, 2 * tq), F32),
                        pltpu.VMEM((DIFF_HEADS, DIFF_V_DIM, 2 * tq), F32)],
        compiler_params=_cparams(2),
        name="diff_attn",
    )(qt, k, vt, lam_params, sub_gain)


def _gelu_tanh(x):
    return 0.5 * x * (1.0 + jnp.tanh(math.sqrt(2.0 / math.pi) * (x + 0.044715 * (x * x * x))))


def _sigmoid(x):
    return 1.0 / (1.0 + jnp.exp(-x))


def _post_kernel(tok_ref, omem_ref, x_ref, wglu_ref, wot_ref, wom_ref, g_ref, wg_ref, wu_ref,
                 wd_ref, o_ref, *, s5):
    tok = tok_ref[...].T
    if s5:
        h = _gelu_tanh(tok)
        gate = _sigmoid(_dot(h.astype(BF16), wglu_ref[...]))
        o_tok = (h * gate).astype(BF16)
    else:
        o_tok = tok.astype(BF16)
    x1 = x_ref[...] + _dot(o_tok, wot_ref[...]) + _dot(omem_ref[...], wom_ref[...])
    h2 = _rms(x1, g_ref[...]).astype(BF16)
    gate = _dot(h2, wg_ref[...])
    up = _dot(h2, wu_ref[...])
    act = (gate * _sigmoid(gate) * up).astype(BF16)
    o_ref[...] = x1 + _dot(act, wd_ref[...])


def _post(tok, o_mem, x, w_glu, w_out_tok, w_out_mem, g, wg, wu, wd, *, s5, tm=256):
    n_tok = x.shape[0]
    return pl.pallas_call(
        functools.partial(_post_kernel, s5=s5),
        grid=(n_tok // tm,),
        in_specs=[
            pl.BlockSpec((TOK_WIDTH, tm), lambda i: (0, i)),
            pl.BlockSpec((tm, MEM_WIDTH), lambda i: (i, 0)),
            pl.BlockSpec((tm, D_MODEL), lambda i: (i, 0)),
            _const_spec(w_glu.shape),
            _const_spec(w_out_tok.shape),
            _const_spec(w_out_mem.shape),
            _const_spec((1, D_MODEL)),
            _const_spec(wg.shape),
            _const_spec(wu.shape),
            _const_spec(wd.shape),
        ],
        out_specs=pl.BlockSpec((tm, D_MODEL), lambda i: (i, 0)),
        out_shape=jax.ShapeDtypeStruct((n_tok, D_MODEL), F32),
        compiler_params=_cparams(1),
        name="post_s5" if s5 else "post",
    )(tok, o_mem, x, w_glu, w_out_tok, w_out_mem, g, wg, wu, wd)


def _tile_gain(g, reps):
    return jnp.tile(g.astype(F32), reps)[None, :]


def kernel(x, mem, norm_mix, norm_ffn, norm_mem, w_in, w_out, mem_wk, mem_wv, mem_q_norm,
           mem_k_norm, ffn_w_gate, ffn_w_up, ffn_w_down, ssm_a_re, ssm_a_im, ssm_log_dt,
           ssm_b_re, ssm_b_im, ssm_c_re, ssm_c_im, ssm_d, ssm_w_glu, kv_norm, diff_wk, diff_wv,
           diff_k_norm, diff_q_norm, diff_lambda_q1, diff_lambda_k1, diff_lambda_q2,
           diff_lambda_k2, diff_sub_norm):
    bsz, seq, _ = x.shape
    n_tok = bsz * seq
    n_chunks = seq // SSM_C
    xf = x.reshape(n_tok, D_MODEL)

    wkv = jnp.concatenate([mem_wk, mem_wv], axis=-1).astype(BF16)
    k_gain = jnp.tile(mem_k_norm.astype(F32), (1, MEM_HEADS))[:, None, :]
    k_mem, v_mem = _mem_kv(mem, norm_mem[:, None, :], wkv, k_gain)

    k_sh = v_sh = None
    for layer in range(DEPTH):
        s5 = layer < N_A_LAYERS
        w_in_l = w_in[layer].astype(BF16)
        q_gain = _tile_gain(mem_q_norm[layer], MEM_HEADS)
        if s5:
            tok_gain = jnp.ones((TOK_WIDTH, 1), F32)
        else:
            tok_gain = _tile_gain(diff_q_norm[layer - N_A_LAYERS], 2 * DIFF_HEADS).T
        z_tok, o_mem = _in_proj(xf, norm_mix[layer][None, :], w_in_l[:, :TOK_WIDTH].T,
                                w_in_l[:, TOK_WIDTH:], tok_gain, q_gain, k_mem[layer],
                                v_mem[layer], seq=seq, qk_norm=not s5)
        if s5:
            i = layer
            log_dt = ssm_log_dt[i][:, None, None]
            a_row = jnp.tile(jnp.stack([ssm_a_re[i], ssm_a_im[i]], axis=1), (1, 1, 2))
            a_col = jnp.stack([ssm_a_re[i], ssm_a_im[i]], axis=2)
            b_t = jnp.tile(jnp.stack([ssm_b_re[i], ssm_b_im[i]], axis=1).transpose(0, 1, 3, 2),
                           (1, 1, 1, 2))
            c_t = jnp.stack([ssm_c_re[i], ssm_c_im[i]], axis=1).transpose(0, 1, 3, 2)
            m_op, w_op, v_op, ac = _ssm_prep(log_dt, a_row, a_col, b_t, c_t)
            d_rep = jnp.repeat(ssm_d[i].reshape(SSM_G, 1, SSM_P), SSM_C, axis=2)
            u_all = (z_tok.reshape(SSM_G, SSM_P, bsz * n_chunks, SSM_C)
                     .transpose(0, 2, 1, 3).reshape(SSM_G, bsz * n_chunks, SSM_K))
            y_all = _ssm_scan(u_all, m_op, w_op, v_op, ac, d_rep, bsz=bsz)
            tok = (y_all.reshape(SSM_G, bsz * n_chunks, SSM_P, SSM_C)
                   .transpose(0, 2, 1, 3).reshape(TOK_WIDTH, n_tok))
            w_glu = ssm_w_glu[i].astype(BF16)
        else:
            j = layer - N_A_LAYERS
            lambda_init = 0.8 - 0.6 * math.exp(-0.3 * layer)
            lam_params = jnp.stack([diff_lambda_q1[j], diff_lambda_k1[j],
                                    diff_lambda_q2[j], diff_lambda_k2[j]]).astype(F32)
            tok = _diff_attn(z_tok, k_sh, v_sh, lam_params,
                             _tile_gain(diff_sub_norm[j], DIFF_HEADS).T,
                             bsz=bsz, seq=seq, lambda_init=lambda_init)
            w_glu = jnp.zeros((8, LANES), BF16)
        w_out_l = w_out[layer].astype(BF16)
        xf = _post(tok, o_mem, xf, w_glu, w_out_l[:TOK_WIDTH], w_out_l[TOK_WIDTH:],
                   norm_ffn[layer][None, :], ffn_w_gate[layer].astype(BF16),
                   ffn_w_up[layer].astype(BF16), ffn_w_down[layer].astype(BF16), s5=s5)
        if layer == N_A_LAYERS - 1:
            k_sh, v_sh = _shared_kv(xf, kv_norm[None, :], diff_wk.astype(BF16),
                                    diff_wv.astype(BF16).T,
                                    _tile_gain(diff_k_norm, 2 * DIFF_HEADS))
    return xf.reshape(bsz, seq, D_MODEL)
```

```python
import functools
import math

import jax
import jax.numpy as jnp
from jax import lax
from jax.experimental import pallas as pl
from jax.experimental.pallas import tpu as pltpu

D_MODEL = 1024
DEPTH = 4
CHUNK = 64
N_A_LAYERS = DEPTH // 2
MEM_HEADS = 4
MEM_HEAD_DIM = 64
MEM_WIDTH = MEM_HEADS * MEM_HEAD_DIM
TOK_WIDTH = D_MODEL - MEM_WIDTH
SSM_P = 16
SSM_G = TOK_WIDTH // SSM_P
SSM_N = 64
SSM_C = 64
SSM_K = SSM_P * SSM_C
DIFF_HEAD_DIM = 64
DIFF_HEADS = TOK_WIDTH // (2 * DIFF_HEAD_DIM)
DIFF_V_DIM = 2 * DIFF_HEAD_DIM
D_FF = 2816
EPS = 1e-6

LANES = 128
VMEM_LIMIT = 56 * 1024 * 1024

F32 = jnp.float32
BF16 = jnp.bfloat16


def _cparams(n_grid, vmem=VMEM_LIMIT, flags=None):
    return pltpu.CompilerParams(
        dimension_semantics=("arbitrary",) * n_grid, vmem_limit_bytes=vmem, flags=flags)


def _const_spec(shape):
    nd = len(shape)
    return pl.BlockSpec(shape, lambda *_: (0,) * nd, pipeline_mode=pl.Buffered(1))


def _rms(x, g):
    return x * lax.rsqrt(jnp.mean(x * x, axis=-1, keepdims=True) + EPS) * g


def _seg_rms64(x, g):
    rows, width = x.shape
    lane = lax.broadcasted_iota(jnp.int32, (rows, LANES), 1)
    lo = lane < 64
    outs = []
    for v in range(width // LANES):
        xv = x[:, v * LANES:(v + 1) * LANES]
        sq = xv * xv
        s_lo = jnp.sum(jnp.where(lo, sq, 0.0), axis=-1, keepdims=True)
        s_hi = jnp.sum(jnp.where(lo, 0.0, sq), axis=-1, keepdims=True)
        r_lo = lax.rsqrt(s_lo * (1.0 / 64) + EPS)
        r_hi = lax.rsqrt(s_hi * (1.0 / 64) + EPS)
        outs.append(xv * jnp.where(lo, r_lo, r_hi) * g[:, v * LANES:(v + 1) * LANES])
    return jnp.concatenate(outs, axis=1)


def _half_masks(rows, dtype):
    lane = lax.broadcasted_iota(jnp.int32, (rows, LANES), 1)
    lo = (lane < 64).astype(dtype)
    return lo, (1 - lo).astype(dtype)


def _dot(a, b):
    return jnp.dot(a, b, preferred_element_type=F32)


def _dot_nt(a, b):
    return lax.dot_general(a, b, (((1,), (1,)), ((), ())), preferred_element_type=F32)


def _mem_kv_kernel(mem_ref, g_ref, wk_ref, wvt_ref, kg_ref, k_ref, vt_ref):
    mn = _rms(mem_ref[0], g_ref[0]).astype(BF16)
    k_ref[0, 0] = _seg_rms64(_dot(mn, wk_ref[0]), kg_ref[0]).astype(BF16)
    vt_ref[0, 0] = _dot_nt(wvt_ref[0], mn).astype(BF16)


def _mem_kv(mem, norm_mem, wk, wv_t, k_gain):
    bsz, n_mem, _ = mem.shape
    return pl.pallas_call(
        _mem_kv_kernel,
        grid=(DEPTH, bsz),
        in_specs=[
            pl.BlockSpec((1, n_mem, D_MODEL), lambda l, b: (b, 0, 0)),
            pl.BlockSpec((1, 1, D_MODEL), lambda l, b: (l, 0, 0)),
            pl.BlockSpec((1, D_MODEL, MEM_WIDTH), lambda l, b: (l, 0, 0)),
            pl.BlockSpec((1, MEM_WIDTH, D_MODEL), lambda l, b: (l, 0, 0)),
            pl.BlockSpec((1, 1, MEM_WIDTH), lambda l, b: (l, 0, 0)),
        ],
        out_specs=[pl.BlockSpec((1, 1, n_mem, MEM_WIDTH), lambda l, b: (l, b, 0, 0)),
                   pl.BlockSpec((1, 1, MEM_WIDTH, n_mem), lambda l, b: (l, b, 0, 0))],
        out_shape=[jax.ShapeDtypeStruct((DEPTH, bsz, n_mem, MEM_WIDTH), BF16),
                   jax.ShapeDtypeStruct((DEPTH, bsz, MEM_WIDTH, n_mem), BF16)],
        compiler_params=_cparams(2),
        name="mem_kv",
    )(mem, norm_mem, wk, wv_t, k_gain)


def _seg_rms_rows(xt, g_col, seg):
    width, cols = xt.shape
    x3 = xt.reshape(width // seg, seg, cols)
    ms = jnp.mean(x3 * x3, axis=1, keepdims=True)
    return (x3 * lax.rsqrt(ms + EPS)).reshape(width, cols) * g_col


ONES_ROWS = 16
SOFTMAX_LOG2E = math.log2(math.e)


def _with_ones_rows(vt):
    return jnp.concatenate([vt, jnp.ones((ONES_ROWS, vt.shape[1]), vt.dtype)], axis=0)


def _mem_attention_t(qt, k, vt):
    tm = qt.shape[1]
    first = lax.broadcasted_iota(jnp.int32, (LANES, tm), 0) < MEM_HEAD_DIM
    outs = []
    for grp in range(MEM_WIDTH // LANES):
        rows = slice(grp * LANES, (grp + 1) * LANES)
        qg = qt[rows, :]
        zero = jnp.zeros_like(qg)
        qm = jnp.concatenate([jnp.where(first, qg, zero), jnp.where(first, zero, qg)], axis=1)
        s = _dot(k[:, rows], qm)
        p = jnp.exp2(s - jnp.max(s, axis=0, keepdims=True))
        o2 = _dot(_with_ones_rows(vt[rows, :]), p.astype(BF16))
        o2 = o2[:LANES] * (1.0 / o2[LANES:LANES + 1])
        outs.append(jnp.where(first, o2[:, :tm], o2[:, tm:]))
    return jnp.concatenate(outs, axis=0)


def _in_proj_kernel(x_ref, g_ref, win_ref, tokg_ref, qg_ref, k_ref, vt_ref,
                    ztok_ref, omem_ref, *, qk_norm):
    h = _rms(x_ref[...], g_ref[...]).astype(BF16)
    zt = _dot_nt(win_ref[...], h)
    z_tok = zt[:TOK_WIDTH]
    if qk_norm:
        z_tok = (_seg_rms_rows(z_tok, tokg_ref[...], DIFF_HEAD_DIM)
                 * (DIFF_HEAD_DIM ** -0.5 * SOFTMAX_LOG2E))
    ztok_ref[...] = z_tok.astype(ztok_ref.dtype)
    q_mem = (_seg_rms_rows(zt[TOK_WIDTH:], qg_ref[...], MEM_HEAD_DIM)
             * (MEM_HEAD_DIM ** -0.5 * SOFTMAX_LOG2E))
    omem_ref[...] = _mem_attention_t(q_mem.astype(BF16), k_ref[0], vt_ref[0]).astype(omem_ref.dtype)


def _in_proj(x, g, w_in_t, tok_gain, q_gain, k_mem, vt_mem, *, seq, qk_norm, tm=512):
    n_tok = x.shape[0]
    n_mem = k_mem.shape[1]
    tiles_per_seq = seq // tm
    return pl.pallas_call(
        functools.partial(_in_proj_kernel, qk_norm=qk_norm),
        grid=(n_tok // tm,),
        in_specs=[
            pl.BlockSpec((tm, D_MODEL), lambda i: (i, 0)),
            _const_spec((1, D_MODEL)),
            _const_spec(w_in_t.shape),
            _const_spec((TOK_WIDTH, 1)),
            _const_spec((MEM_WIDTH, 1)),
            pl.BlockSpec((1, n_mem, MEM_WIDTH), lambda i: (i // tiles_per_seq, 0, 0)),
            pl.BlockSpec((1, MEM_WIDTH, n_mem), lambda i: (i // tiles_per_seq, 0, 0)),
        ],
        out_specs=[pl.BlockSpec((TOK_WIDTH, tm), lambda i: (0, i)),
                   pl.BlockSpec((MEM_WIDTH, tm), lambda i: (0, i))],
        out_shape=[jax.ShapeDtypeStruct((TOK_WIDTH, n_tok), BF16 if qk_norm else F32),
                   jax.ShapeDtypeStruct((MEM_WIDTH, n_tok), F32)],
        compiler_params=_cparams(1),
        name="in_proj_q" if qk_norm else "in_proj_u",
    )(x, g, w_in_t, tok_gain, q_gain, k_mem, vt_mem)


def _cpow(ar, ai, dt, k):
    mag = jnp.exp(k * dt * ar)
    ang = k * dt * ai
    return mag * jnp.cos(ang), mag * jnp.sin(ang)


def _zoh_coef(ar, ai, dt):
    abr, abi = _cpow(ar, ai, dt, 1.0)
    den = ar * ar + ai * ai
    nr = abr - 1.0
    return (nr * ar + abi * ai) / den, (abi * ar - nr * ai) / den, abr, abi


def _ssm_prep_kernel(ldt_ref, arow_ref, acol_ref, bt_ref, ct_ref, m_ref, w_ref, v_ref, ac_ref):
    dt = jnp.exp(ldt_ref[0])
    first = lax.broadcasted_iota(jnp.int32, (1, LANES), 1) < SSM_N
    ar_r, ai_r = arow_ref[0, 0:1, :], arow_ref[0, 1:2, :]
    ar_c, ai_c = acol_ref[0, :, 0:1], acol_ref[0, :, 1:2]
    cf_re, cf_im, _, _ = _zoh_coef(ar_r, ai_r, dt)
    ab_re_c, ab_im_c = _cpow(ar_c, ai_c, dt, 1.0)

    pc_re, pc_im = _cpow(ar_r, ai_r, dt, float(SSM_C))
    ac_ref[0, 0:1, :] = pc_re
    ac_ref[0, 1:2, :] = jnp.where(first, -pc_im, pc_im)

    bre, bim = bt_ref[0, 0], bt_ref[0, 1]
    bb_re = cf_re * bre - cf_im * bim
    bb_im = cf_re * bim + cf_im * bre

    s_row = lax.broadcasted_iota(jnp.int32, (SSM_C, LANES), 0).astype(F32)
    e_re, e_im = _cpow(ar_r, ai_r, dt, float(SSM_C - 1) - s_row)
    for p in range(SSM_P):
        br, bi = bb_re[p:p + 1, :], bb_im[p:p + 1, :]
        w_ref[0, p * SSM_C:(p + 1) * SSM_C, :] = jnp.where(
            first, br * e_re - bi * e_im, br * e_im + bi * e_re).astype(BF16)

    lane_n = lax.broadcasted_iota(jnp.int32, (SSM_N, LANES), 1)
    tau = (lane_n % SSM_C).astype(F32)
    p0_re, p0_im = _cpow(ar_c, ai_c, dt, tau)
    p1_re = p0_re * ab_re_c - p0_im * ab_im_c
    p1_im = p0_re * ab_im_c + p0_im * ab_re_c
    reps = SSM_K // LANES
    p0_re, p0_im = jnp.concatenate([p0_re] * reps, 1), jnp.concatenate([p0_im] * reps, 1)
    p1_re, p1_im = jnp.concatenate([p1_re] * reps, 1), jnp.concatenate([p1_im] * reps, 1)

    def expand(ct):
        return jnp.concatenate(
            [jnp.where(lane_n < SSM_C, ct[:, 2 * j:2 * j + 1], ct[:, 2 * j + 1:2 * j + 2])
             for j in range(SSM_P // 2)], axis=1)

    c_re, c_im = expand(ct_ref[0, 0]), expand(ct_ref[0, 1])

    v_ref[0, 0:SSM_N, :] = (c_re * p1_re - c_im * p1_im).astype(BF16)
    v_ref[0, SSM_N:2 * SSM_N, :] = (-(c_re * p1_im + c_im * p1_re)).astype(BF16)

    ce = jnp.concatenate([c_re * p0_re - c_im * p0_im, -(c_re * p0_im + c_im * p0_re)], axis=0)
    kt = jnp.dot(jnp.where(first, bb_re, bb_im), ce, precision=lax.Precision.HIGHEST,
                 preferred_element_type=F32)

    row = lax.broadcasted_iota(jnp.int32, (SSM_C, SSM_K), 0)
    lane_t = lax.broadcasted_iota(jnp.int32, (SSM_C, SSM_K), 1) % SSM_C
    causal = lane_t >= row
    for p in range(SSM_P):
        base = jnp.broadcast_to(kt[p:p + 1, :], (SSM_C, SSM_K))
        shifted = pltpu.roll(base, 0, 1, stride=1, stride_axis=0)
        m_ref[0, p * SSM_C:(p + 1) * SSM_C, :] = jnp.where(causal, shifted, 0.0).astype(BF16)


def _ssm_prep(log_dt, a_row, a_col, b_t, c_t):
    g = log_dt.shape[0]
    return pl.pallas_call(
        _ssm_prep_kernel,
        grid=(g,),
        in_specs=[
            pl.BlockSpec((1, 1, 1), lambda i: (i, 0, 0)),
            pl.BlockSpec((1, 2, 2 * SSM_N), lambda i: (i, 0, 0)),
            pl.BlockSpec((1, SSM_N, 2), lambda i: (i, 0, 0)),
            pl.BlockSpec((1, 2, SSM_P, 2 * SSM_N), lambda i: (i, 0, 0, 0)),
            pl.BlockSpec((1, 2, SSM_N, SSM_P), lambda i: (i, 0, 0, 0)),
        ],
        out_specs=[
            pl.BlockSpec((1, SSM_K, SSM_K), lambda i: (i, 0, 0)),
            pl.BlockSpec((1, SSM_K, 2 * SSM_N), lambda i: (i, 0, 0)),
            pl.BlockSpec((1, 2 * SSM_N, SSM_K), lambda i: (i, 0, 0)),
            pl.BlockSpec((1, 2, 2 * SSM_N), lambda i: (i, 0, 0)),
        ],
        out_shape=[
            jax.ShapeDtypeStruct((g, SSM_K, SSM_K), BF16),
            jax.ShapeDtypeStruct((g, SSM_K, 2 * SSM_N), BF16),
            jax.ShapeDtypeStruct((g, 2 * SSM_N, SSM_K), BF16),
            jax.ShapeDtypeStruct((g, 2, 2 * SSM_N), F32),
        ],
        compiler_params=_cparams(1),
        name="ssm_prep",
    )(log_dt, a_row, a_col, b_t, c_t)


def _ssm_scan_kernel(u_ref, m_ref, w_ref, v_ref, ac_ref, d_ref, y_ref, z_scr, zs_scr, sp_scr,
                     *, bsz, n_chunks):
    pairs = n_chunks // 2
    n_pairs = bsz * pairs
    ut = u_ref[...]
    u2 = ut.reshape(SSM_P, n_pairs, LANES).transpose(1, 0, 2).reshape(n_pairs, SSM_P * LANES)
    low = lax.broadcasted_iota(jnp.int32, (n_pairs, LANES), 1) < SSM_C

    def col(a, j):
        return a[:, j * LANES:(j + 1) * LANES]

    def swap(a):
        return pltpu.roll(a, SSM_C, 1)

    even = [jnp.where(low, col(u2, 2 * k), swap(col(u2, 2 * k + 1))) for k in range(SSM_P // 2)]
    odd = [jnp.where(low, swap(col(u2, 2 * k)), col(u2, 2 * k + 1)) for k in range(SSM_P // 2)]
    ub = jnp.concatenate([jnp.concatenate(even, axis=1), jnp.concatenate(odd, axis=1)],
                         axis=0).astype(BF16)
    y = _dot(ub, m_ref[0])
    z = _dot(ub, w_ref[0])
    z_scr[...] = z
    zs_scr[...] = pltpu.roll(z, SSM_N, 1)

    a1 = ac_ref[0, 0:1, :]
    a2 = ac_ref[0, 1:2, :]
    s = jnp.zeros((bsz, 2 * SSM_N), F32)
    ss = jnp.zeros((bsz, 2 * SSM_N), F32)
    for c in range(n_chunks):
        rows = pl.ds((c % 2) * n_pairs + c // 2, bsz, stride=pairs)
        sp_scr[rows, :] = s
        if c + 1 < n_chunks:
            s, ss = (a1 * s + a2 * ss + z_scr[rows, :],
                     a1 * ss - a2 * s + zs_scr[rows, :])
    y = y + _dot(sp_scr[...].astype(BF16), v_ref[0])
    y_even, y_odd = y[:n_pairs], y[n_pairs:]
    cols = []
    for k in range(SSM_P // 2):
        ce, co = col(y_even, k), col(y_odd, k)
        cols += [jnp.where(low, ce, swap(co)), jnp.where(low, swap(ce), co)]
    y2 = jnp.concatenate(cols, axis=1)
    yt = y2.reshape(n_pairs, SSM_P, LANES).transpose(1, 0, 2).reshape(SSM_P, n_pairs * LANES)
    y_ref[...] = yt + d_ref[...] * ut


def _ssm_scan(zt, m, w, v, ac, d_col, *, bsz):
    n_tok = zt.shape[1]
    n_rows = n_tok // SSM_C
    n_chunks = n_rows // bsz
    kern = functools.partial(_ssm_scan_kernel, bsz=bsz, n_chunks=n_chunks)
    return pl.pallas_call(
        kern,
        grid=(SSM_G,),
        in_specs=[
            pl.BlockSpec((SSM_P, n_tok), lambda i: (i, 0)),
            pl.BlockSpec((1, SSM_K, SSM_K), lambda i: (i, 0, 0)),
            pl.BlockSpec((1, SSM_K, 2 * SSM_N), lambda i: (i, 0, 0)),
            pl.BlockSpec((1, 2 * SSM_N, SSM_K), lambda i: (i, 0, 0)),
            pl.BlockSpec((1, 2, 2 * SSM_N), lambda i: (i, 0, 0)),
            pl.BlockSpec((SSM_P, 1), lambda i: (i, 0)),
        ],
        out_specs=pl.BlockSpec((SSM_P, n_tok), lambda i: (i, 0)),
        out_shape=jax.ShapeDtypeStruct((TOK_WIDTH, n_tok), F32),
        scratch_shapes=[pltpu.VMEM((n_rows, 2 * SSM_N), F32)] * 3,
        compiler_params=_cparams(1),
        name="ssm_scan",
    )(zt, m, w, v, ac, d_col)


ATTN_BLOCK = 256


def _shared_kv_kernel(x_ref, g_ref, wk_ref, wvt_ref, kg_ref, k_ref, vt_ref):
    h = _rms(x_ref[...], g_ref[...]).astype(BF16)
    k_ref[...] = _seg_rms64(_dot(h, wk_ref[...]), kg_ref[...]).astype(BF16)
    vt_ref[0] = _dot_nt(wvt_ref[...], h).astype(BF16)


def _shared_kv(x, g, wk, wv_t, k_gain):
    n_tok = x.shape[0]
    tm = ATTN_BLOCK
    return pl.pallas_call(
        _shared_kv_kernel,
        grid=(n_tok // tm,),
        in_specs=[
            pl.BlockSpec((tm, D_MODEL), lambda i: (i, 0)),
            _const_spec((1, D_MODEL)),
            _const_spec(wk.shape),
            _const_spec(wv_t.shape),
            _const_spec((1, TOK_WIDTH)),
        ],
        out_specs=[pl.BlockSpec((tm, TOK_WIDTH), lambda i: (i, 0)),
                   pl.BlockSpec((1, TOK_WIDTH, tm), lambda i: (i, 0, 0))],
        out_shape=[jax.ShapeDtypeStruct((n_tok, TOK_WIDTH), BF16),
                   jax.ShapeDtypeStruct((n_tok // tm, TOK_WIDTH, tm), BF16)],
        compiler_params=_cparams(1),
        name="shared_kv",
    )(x, g, wk, wv_t, k_gain)


N_STREAMS = 2 * DIFF_HEADS


def _diff_attn_kernel(qt_ref, k_ref, vt_ref, lam_ref, subg_ref, o_ref, qm_scr, sc_scr, p_scr,
                      bm_scr, m_scr, al_scr, acc_scr, *, lambda_init):
    tq = ATTN_BLOCK
    qi = pl.program_id(1)
    lp = lam_ref[...]
    lam = (jnp.exp(jnp.sum(lp[0:1] * lp[1:2], axis=-1, keepdims=True))
           - jnp.exp(jnp.sum(lp[2:3] * lp[3:4], axis=-1, keepdims=True)) + lambda_init)
    key_chunk = lax.broadcasted_iota(jnp.int32, (tq, tq), 0) // CHUNK
    qry_chunk = lax.broadcasted_iota(jnp.int32, (tq, tq), 1) // CHUNK
    diag_mask = key_chunk <= qry_chunk
    first_half = lax.broadcasted_iota(jnp.int32, (LANES, tq), 0) < DIFF_HEAD_DIM
    n_grp = DIFF_HEADS // 2

    for kg in range(2 * n_grp):
        qg = qt_ref[kg * LANES:(kg + 1) * LANES, :]
        zero = jnp.zeros_like(qg)
        qm_scr[kg, :, 0:tq] = jnp.where(first_half, qg, zero)
        qm_scr[kg, :, tq:2 * tq] = jnp.where(first_half, zero, qg)

    def block(j, diagonal):
        rows = pl.ds(pl.multiple_of(j * tq, tq), tq)
        for kg in range(2 * n_grp):
            kb = k_ref[rows, kg * LANES:(kg + 1) * LANES]
            st = _dot(kb, qm_scr[kg])
            if diagonal:
                st = jnp.where(jnp.concatenate([diag_mask, diag_mask], axis=1), st, -jnp.inf)
            sc_scr[kg] = st
            bm_scr[kg] = jnp.max(st, axis=0, keepdims=True)
        for kg in range(2 * n_grp):
            comp, g = divmod(kg, n_grp)
            if diagonal:
                m_new = bm_scr[kg]
            else:
                m_old = m_scr[kg]
                m_new = jnp.maximum(m_old, bm_scr[kg])
                alpha = jnp.exp2(m_old - m_new)
            pb = jnp.exp2(sc_scr[kg] - m_new).astype(BF16)
            m_scr[kg] = m_new
            for hh in range(2):
                c_cols = slice(comp * tq, (comp + 1) * tq)
                if not diagonal:
                    al_scr[2 * g + hh, :, c_cols] = alpha[:, hh * tq:(hh + 1) * tq]
                p_scr[2 * g + hh, :, c_cols] = pb[:, hh * tq:(hh + 1) * tq]
        for head in range(DIFF_HEADS):
            vt = vt_ref[j, head * DIFF_V_DIM:(head + 1) * DIFF_V_DIM, :]
            pv = _dot(_with_ones_rows(vt), p_scr[head])
            acc_scr[head] = pv if diagonal else al_scr[head] * acc_scr[head] + pv

    def body(j, carry):
        block(j, False)
        return carry

    block(qi, True)
    lax.fori_loop(0, qi, body, 0)

    for head in range(DIFF_HEADS):
        rows = slice(head * DIFF_V_DIM, (head + 1) * DIFF_V_DIM)
        denom = acc_scr[head, DIFF_V_DIM:DIFF_V_DIM + 1, :]
        on = acc_scr[head, 0:DIFF_V_DIM, :] * (1.0 / denom)
        o = on[:, 0:tq] - lam * on[:, tq:2 * tq]
        o = o * lax.rsqrt(jnp.mean(o * o, axis=0, keepdims=True) + EPS)
        o_ref[rows, :] = (o * subg_ref[rows, :] * (1.0 - lambda_init)).astype(o_ref.dtype)


def _diff_attn(qt, k, vt, lam_params, sub_gain, *, bsz, seq, lambda_init):
    tq = ATTN_BLOCK
    n_q = seq // tq
    return pl.pallas_call(
        functools.partial(_diff_attn_kernel, lambda_init=lambda_init),
        grid=(bsz, n_q),
        in_specs=[
            pl.BlockSpec((TOK_WIDTH, tq), lambda b, i: (0, b * n_q + i)),
            pl.BlockSpec((seq, TOK_WIDTH), lambda b, i: (b, 0)),
            pl.BlockSpec((n_q, TOK_WIDTH, tq), lambda b, i: (b, 0, 0)),
            pl.BlockSpec((4, DIFF_HEAD_DIM), lambda b, i: (0, 0)),
            pl.BlockSpec((TOK_WIDTH, 1), lambda b, i: (0, 0)),
        ],
        out_specs=pl.BlockSpec((TOK_WIDTH, tq), lambda b, i: (0, b * n_q + i)),
        out_shape=jax.ShapeDtypeStruct((TOK_WIDTH, bsz * seq), F32),
        scratch_shapes=[pltpu.VMEM((DIFF_HEADS, LANES, 2 * tq), BF16),
                        pltpu.VMEM((DIFF_HEADS, tq, 2 * tq), F32),
                        pltpu.VMEM((DIFF_HEADS, tq, 2 * tq), BF16),
                        pltpu.VMEM((DIFF_HEADS, 1, 2 * tq), F32),
                        pltpu.VMEM((DIFF_HEADS, 1, 2 * tq), F32),
                        pltpu.VMEM((DIFF_HEADS, 1, 2 * tq), F32),
                        pltpu.VMEM((DIFF_HEADS, DIFF_V_DIM + ONES_ROWS, 2 * tq), F32)],
        compiler_params=_cparams(2),
        name="diff_attn",
    )(qt, k, vt, lam_params, sub_gain)


def _gelu_tanh(x):
    return 0.5 * x * (1.0 + jnp.tanh(math.sqrt(2.0 / math.pi) * (x + 0.044715 * (x * x * x))))


def _sigmoid(x):
    return 1.0 / (1.0 + jnp.exp(-x))


def _post_kernel(tok_ref, omem_ref, x_ref, wglu_ref, wout_ref, g_ref, wg_ref, wu_ref,
                 wd_ref, o_ref, *, s5):
    tok = tok_ref[...].T
    if s5:
        h = _gelu_tanh(tok)
        gate = _sigmoid(_dot(h.astype(BF16), wglu_ref[...]))
        tok = h * gate
    mix = jnp.concatenate([tok.astype(BF16), omem_ref[...].T.astype(BF16)], axis=1)
    x1 = x_ref[...] + _dot(mix, wout_ref[...])
    h2 = _rms(x1, g_ref[...]).astype(BF16)
    gate = _dot(h2, wg_ref[...])
    up = _dot(h2, wu_ref[...])
    act = (gate * _sigmoid(gate) * up).astype(BF16)
    o_ref[...] = x1 + _dot(act, wd_ref[...])


def _post(tok, o_mem, x, w_glu, w_out, g, wg, wu, wd, *, s5, tm=512):
    n_tok = x.shape[0]
    return pl.pallas_call(
        functools.partial(_post_kernel, s5=s5),
        grid=(n_tok // tm,),
        in_specs=[
            pl.BlockSpec((TOK_WIDTH, tm), lambda i: (0, i)),
            pl.BlockSpec((MEM_WIDTH, tm), lambda i: (0, i)),
            pl.BlockSpec((tm, D_MODEL), lambda i: (i, 0)),
            _const_spec(w_glu.shape),
            _const_spec(w_out.shape),
            _const_spec((1, D_MODEL)),
            _const_spec(wg.shape),
            _const_spec(wu.shape),
            _const_spec(wd.shape),
        ],
        out_specs=pl.BlockSpec((tm, D_MODEL), lambda i: (i, 0)),
        out_shape=jax.ShapeDtypeStruct((n_tok, D_MODEL), F32),
        compiler_params=_cparams(1),
        name="post_s5" if s5 else "post",
    )(tok, o_mem, x, w_glu, w_out, g, wg, wu, wd)


def _tile_gain(g, reps):
    return jnp.tile(g.astype(F32), reps)[None, :]


def kernel(x, mem, norm_mix, norm_ffn, norm_mem, w_in, w_out, mem_wk, mem_wv, mem_q_norm,
           mem_k_norm, ffn_w_gate, ffn_w_up, ffn_w_down, ssm_a_re, ssm_a_im, ssm_log_dt,
           ssm_b_re, ssm_b_im, ssm_c_re, ssm_c_im, ssm_d, ssm_w_glu, kv_norm, diff_wk, diff_wv,
           diff_k_norm, diff_q_norm, diff_lambda_q1, diff_lambda_k1, diff_lambda_q2,
           diff_lambda_k2, diff_sub_norm):
    bsz, seq, _ = x.shape
    n_tok = bsz * seq
    xf = x.reshape(n_tok, D_MODEL)

    k_gain = jnp.tile(mem_k_norm.astype(F32), (1, MEM_HEADS))[:, None, :]
    k_mem, vt_mem = _mem_kv(mem, norm_mem[:, None, :], mem_wk.astype(BF16),
                            mem_wv.astype(BF16).transpose(0, 2, 1), k_gain)

    k_sh = v_sh = None
    for layer in range(DEPTH):
        s5 = layer < N_A_LAYERS
        q_gain = _tile_gain(mem_q_norm[layer], MEM_HEADS).T
        if s5:
            tok_gain = jnp.ones((TOK_WIDTH, 1), F32)
        else:
            tok_gain = _tile_gain(diff_q_norm[layer - N_A_LAYERS], 2 * DIFF_HEADS).T
        z_tok, o_mem = _in_proj(xf, norm_mix[layer][None, :], w_in[layer].astype(BF16).T,
                                tok_gain, q_gain, k_mem[layer], vt_mem[layer],
                                seq=seq, qk_norm=not s5)
        if s5:
            i = layer
            log_dt = ssm_log_dt[i][:, None, None]
            a_row = jnp.tile(jnp.stack([ssm_a_re[i], ssm_a_im[i]], axis=1), (1, 1, 2))
            a_col = jnp.stack([ssm_a_re[i], ssm_a_im[i]], axis=2)
            b_t = jnp.tile(jnp.stack([ssm_b_re[i], ssm_b_im[i]], axis=1).transpose(0, 1, 3, 2),
                           (1, 1, 1, 2))
            c_t = jnp.stack([ssm_c_re[i], ssm_c_im[i]], axis=1).transpose(0, 1, 3, 2)
            m_op, w_op, v_op, ac = _ssm_prep(log_dt, a_row, a_col, b_t, c_t)
            tok = _ssm_scan(z_tok, m_op, w_op, v_op, ac, ssm_d[i][:, None], bsz=bsz)
            w_glu = ssm_w_glu[i].astype(BF16)
        else:
            j = layer - N_A_LAYERS
            lambda_init = 0.8 - 0.6 * math.exp(-0.3 * layer)
            lam_params = jnp.stack([diff_lambda_q1[j], diff_lambda_k1[j],
                                    diff_lambda_q2[j], diff_lambda_k2[j]]).astype(F32)
            tok = _diff_attn(z_tok, k_sh, v_sh, lam_params,
                             _tile_gain(diff_sub_norm[j], DIFF_HEADS).T,
                             bsz=bsz, seq=seq, lambda_init=lambda_init)
            w_glu = jnp.zeros((8, LANES), BF16)
        xf = _post(tok, o_mem, xf, w_glu, w_out[layer].astype(BF16),
                   norm_ffn[layer][None, :], ffn_w_gate[layer].astype(BF16),
                   ffn_w_up[layer].astype(BF16), ffn_w_down[layer].astype(BF16), s5=s5)
        if layer == N_A_LAYERS - 1:
            k_sh, v_sh = _shared_kv(xf, kv_norm[None, :], diff_wk.astype(BF16),
                                    diff_wv.astype(BF16).T,
                                    _tile_gain(diff_k_norm, 2 * DIFF_HEADS))
    return xf.reshape(bsz, seq, D_MODEL)
```

```python
import functools
import math

import jax
import jax.numpy as jnp
from jax import lax
from jax.experimental import pallas as pl
from jax.experimental.pallas import tpu as pltpu

D_MODEL = 1024
DEPTH = 4
CHUNK = 64
N_A_LAYERS = DEPTH // 2
MEM_HEADS = 4
MEM_HEAD_DIM = 64
MEM_WIDTH = MEM_HEADS * MEM_HEAD_DIM
TOK_WIDTH = D_MODEL - MEM_WIDTH
SSM_P = 16
SSM_G = TOK_WIDTH // SSM_P
SSM_N = 64
SSM_C = 64
SSM_K = SSM_P * SSM_C
DIFF_HEAD_DIM = 64
DIFF_HEADS = TOK_WIDTH // (2 * DIFF_HEAD_DIM)
DIFF_V_DIM = 2 * DIFF_HEAD_DIM
D_FF = 2816
EPS = 1e-6

LANES = 128
VMEM_LIMIT = 56 * 1024 * 1024

F32 = jnp.float32
BF16 = jnp.bfloat16


def _cparams(n_grid, vmem=VMEM_LIMIT, flags=None):
    return pltpu.CompilerParams(
        dimension_semantics=("arbitrary",) * n_grid, vmem_limit_bytes=vmem, flags=flags)


def _const_spec(shape):
    nd = len(shape)
    return pl.BlockSpec(shape, lambda *_: (0,) * nd, pipeline_mode=pl.Buffered(1))


def _rms(x, g):
    return x * lax.rsqrt(jnp.mean(x * x, axis=-1, keepdims=True) + EPS) * g


def _seg_rms64(x, g):
    rows, width = x.shape
    lane = lax.broadcasted_iota(jnp.int32, (rows, LANES), 1)
    lo = lane < 64
    outs = []
    for v in range(width // LANES):
        xv = x[:, v * LANES:(v + 1) * LANES]
        sq = xv * xv
        s_lo = jnp.sum(jnp.where(lo, sq, 0.0), axis=-1, keepdims=True)
        s_hi = jnp.sum(jnp.where(lo, 0.0, sq), axis=-1, keepdims=True)
        r_lo = lax.rsqrt(s_lo * (1.0 / 64) + EPS)
        r_hi = lax.rsqrt(s_hi * (1.0 / 64) + EPS)
        outs.append(xv * jnp.where(lo, r_lo, r_hi) * g[:, v * LANES:(v + 1) * LANES])
    return jnp.concatenate(outs, axis=1)


def _half_masks(rows, dtype):
    lane = lax.broadcasted_iota(jnp.int32, (rows, LANES), 1)
    lo = (lane < 64).astype(dtype)
    return lo, (1 - lo).astype(dtype)


def _dot(a, b):
    return jnp.dot(a, b, preferred_element_type=F32)


def _dot_nt(a, b):
    return lax.dot_general(a, b, (((1,), (1,)), ((), ())), preferred_element_type=F32)


def _mem_kv_kernel(mem_ref, g_ref, wk_ref, wvt_ref, kg_ref, k_ref, vt_ref):
    mn = _rms(mem_ref[0], g_ref[0]).astype(BF16)
    k_ref[0, 0] = _seg_rms64(_dot(mn, wk_ref[0]), kg_ref[0]).astype(BF16)
    vt_ref[0, 0] = _dot_nt(wvt_ref[0], mn).astype(BF16)


def _mem_kv(mem, norm_mem, wk, wv_t, k_gain):
    bsz, n_mem, _ = mem.shape
    return pl.pallas_call(
        _mem_kv_kernel,
        grid=(DEPTH, bsz),
        in_specs=[
            pl.BlockSpec((1, n_mem, D_MODEL), lambda l, b: (b, 0, 0)),
            pl.BlockSpec((1, 1, D_MODEL), lambda l, b: (l, 0, 0)),
            pl.BlockSpec((1, D_MODEL, MEM_WIDTH), lambda l, b: (l, 0, 0)),
            pl.BlockSpec((1, MEM_WIDTH, D_MODEL), lambda l, b: (l, 0, 0)),
            pl.BlockSpec((1, 1, MEM_WIDTH), lambda l, b: (l, 0, 0)),
        ],
        out_specs=[pl.BlockSpec((1, 1, n_mem, MEM_WIDTH), lambda l, b: (l, b, 0, 0)),
                   pl.BlockSpec((1, 1, MEM_WIDTH, n_mem), lambda l, b: (l, b, 0, 0))],
        out_shape=[jax.ShapeDtypeStruct((DEPTH, bsz, n_mem, MEM_WIDTH), BF16),
                   jax.ShapeDtypeStruct((DEPTH, bsz, MEM_WIDTH, n_mem), BF16)],
        compiler_params=_cparams(2),
        name="mem_kv",
    )(mem, norm_mem, wk, wv_t, k_gain)


def _seg_rms_rows(xt, g_col, seg):
    width, cols = xt.shape
    x3 = xt.reshape(width // seg, seg, cols)
    ms = jnp.mean(x3 * x3, axis=1, keepdims=True)
    return (x3 * lax.rsqrt(ms + EPS)).reshape(width, cols) * g_col


ONES_ROWS = 16
SOFTMAX_LOG2E = math.log2(math.e)


def _with_ones_rows(vt):
    return jnp.concatenate([vt, jnp.ones((ONES_ROWS, vt.shape[1]), vt.dtype)], axis=0)


def _mem_attention_t(qt, k, vt):
    tm = qt.shape[1]
    first = lax.broadcasted_iota(jnp.int32, (LANES, tm), 0) < MEM_HEAD_DIM
    outs = []
    for grp in range(MEM_WIDTH // LANES):
        rows = slice(grp * LANES, (grp + 1) * LANES)
        qg = qt[rows, :]
        zero = jnp.zeros_like(qg)
        qm = jnp.concatenate([jnp.where(first, qg, zero), jnp.where(first, zero, qg)], axis=1)
        s = _dot(k[:, rows], qm)
        p = jnp.exp2(s - jnp.max(s, axis=0, keepdims=True))
        o2 = _dot(_with_ones_rows(vt[rows, :]), p.astype(BF16))
        o2 = o2[:LANES] * (1.0 / o2[LANES:LANES + 1])
        outs.append(jnp.where(first, o2[:, :tm], o2[:, tm:]))
    return jnp.concatenate(outs, axis=0)


def _in_proj_kernel(x_ref, g_ref, win_ref, tokg_ref, qg_ref, k_ref, vt_ref,
                    ztok_ref, omem_ref, *, qk_norm):
    h = _rms(x_ref[...], g_ref[...]).astype(BF16)
    zt = _dot_nt(win_ref[...], h)
    z_tok = zt[:TOK_WIDTH]
    if qk_norm:
        z_tok = (_seg_rms_rows(z_tok, tokg_ref[...], DIFF_HEAD_DIM)
                 * (DIFF_HEAD_DIM ** -0.5 * SOFTMAX_LOG2E))
    ztok_ref[...] = z_tok.astype(ztok_ref.dtype)
    q_mem = (_seg_rms_rows(zt[TOK_WIDTH:], qg_ref[...], MEM_HEAD_DIM)
             * (MEM_HEAD_DIM ** -0.5 * SOFTMAX_LOG2E))
    omem_ref[...] = _mem_attention_t(q_mem.astype(BF16), k_ref[0], vt_ref[0]).astype(omem_ref.dtype)


def _in_proj(x, g, w_in_t, tok_gain, q_gain, k_mem, vt_mem, *, seq, qk_norm, tm=512):
    n_tok = x.shape[0]
    n_mem = k_mem.shape[1]
    tiles_per_seq = seq // tm
    return pl.pallas_call(
        functools.partial(_in_proj_kernel, qk_norm=qk_norm),
        grid=(n_tok // tm,),
        in_specs=[
            pl.BlockSpec((tm, D_MODEL), lambda i: (i, 0)),
            _const_spec((1, D_MODEL)),
            _const_spec(w_in_t.shape),
            _const_spec((TOK_WIDTH, 1)),
            _const_spec((MEM_WIDTH, 1)),
            pl.BlockSpec((1, n_mem, MEM_WIDTH), lambda i: (i // tiles_per_seq, 0, 0)),
            pl.BlockSpec((1, MEM_WIDTH, n_mem), lambda i: (i // tiles_per_seq, 0, 0)),
        ],
        out_specs=[pl.BlockSpec((TOK_WIDTH, tm), lambda i: (0, i)),
                   pl.BlockSpec((MEM_WIDTH, tm), lambda i: (0, i))],
        out_shape=[jax.ShapeDtypeStruct((TOK_WIDTH, n_tok), BF16 if qk_norm else F32),
                   jax.ShapeDtypeStruct((MEM_WIDTH, n_tok), F32)],
        compiler_params=_cparams(1),
        name="in_proj_q" if qk_norm else "in_proj_u",
    )(x, g, w_in_t, tok_gain, q_gain, k_mem, vt_mem)


def _cpow(ar, ai, dt, k):
    mag = jnp.exp(k * dt * ar)
    ang = k * dt * ai
    return mag * jnp.cos(ang), mag * jnp.sin(ang)


def _zoh_coef(ar, ai, abr, abi):
    den = ar * ar + ai * ai
    nr = abr - 1.0
    return (nr * ar + abi * ai) / den, (abi * ar - nr * ai) / den


def _ssm_prep_kernel(ldt_ref, arow_ref, acol_ref, bt_ref, ct_ref, m_ref, w_ref, v_ref, ac_ref):
    dt = jnp.exp(ldt_ref[0])
    first = lax.broadcasted_iota(jnp.int32, (1, LANES), 1) < SSM_N
    ar_r, ai_r = arow_ref[0, 0:1, :], arow_ref[0, 1:2, :]
    ar_c, ai_c = acol_ref[0, :, 0:1], acol_ref[0, :, 1:2]

    s_row = lax.broadcasted_iota(jnp.int32, (SSM_C, LANES), 0).astype(F32)
    e_re, e_im = _cpow(ar_r, ai_r, dt, float(SSM_C - 1) - s_row)
    ab_re, ab_im = e_re[SSM_C - 2:SSM_C - 1], e_im[SSM_C - 2:SSM_C - 1]
    cf_re, cf_im = _zoh_coef(ar_r, ai_r, ab_re, ab_im)

    pc_re = e_re[0:1] * ab_re - e_im[0:1] * ab_im
    pc_im = e_re[0:1] * ab_im + e_im[0:1] * ab_re
    ac_ref[0, 0:1, :] = pc_re
    ac_ref[0, 1:2, :] = jnp.where(first, -pc_im, pc_im)

    bre, bim = bt_ref[0, 0], bt_ref[0, 1]
    bb_re = cf_re * bre - cf_im * bim
    bb_im = cf_re * bim + cf_im * bre

    for p in range(SSM_P):
        br, bi = bb_re[p:p + 1, :], bb_im[p:p + 1, :]
        w_ref[0, p * SSM_C:(p + 1) * SSM_C, :] = jnp.where(
            first, br * e_re - bi * e_im, br * e_im + bi * e_re).astype(BF16)

    lane_n = lax.broadcasted_iota(jnp.int32, (SSM_N, LANES), 1)
    tau = (lane_n % SSM_C).astype(F32)
    p0_re, p0_im = _cpow(ar_c, ai_c, dt, tau)
    ab_re_c, ab_im_c = p0_re[:, 1:2], p0_im[:, 1:2]
    p1_re = p0_re * ab_re_c - p0_im * ab_im_c
    p1_im = p0_re * ab_im_c + p0_im * ab_re_c
    reps = SSM_K // LANES
    p0_re, p0_im = jnp.concatenate([p0_re] * reps, 1), jnp.concatenate([p0_im] * reps, 1)
    p1_re, p1_im = jnp.concatenate([p1_re] * reps, 1), jnp.concatenate([p1_im] * reps, 1)

    sel = (lax.broadcasted_iota(jnp.int32, (SSM_P, SSM_K), 1) // SSM_C
           == lax.broadcasted_iota(jnp.int32, (SSM_P, SSM_K), 0)).astype(F32)

    def expand(ct):
        return jnp.dot(ct, sel, precision=lax.Precision.HIGHEST, preferred_element_type=F32)

    c_re, c_im = expand(ct_ref[0, 0]), expand(ct_ref[0, 1])

    v_ref[0, 0:SSM_N, :] = (c_re * p1_re - c_im * p1_im).astype(BF16)
    v_ref[0, SSM_N:2 * SSM_N, :] = (-(c_re * p1_im + c_im * p1_re)).astype(BF16)

    ce = jnp.concatenate([c_re * p0_re - c_im * p0_im, -(c_re * p0_im + c_im * p0_re)], axis=0)
    kt = jnp.dot(jnp.where(first, bb_re, bb_im), ce, precision=lax.Precision.HIGHEST,
                 preferred_element_type=F32)

    row = lax.broadcasted_iota(jnp.int32, (SSM_C, SSM_K), 0)
    lane_t = lax.broadcasted_iota(jnp.int32, (SSM_C, SSM_K), 1) % SSM_C
    causal = lane_t >= row
    for p in range(SSM_P):
        base = jnp.broadcast_to(kt[p:p + 1, :], (SSM_C, SSM_K))
        shifted = pltpu.roll(base, 0, 1, stride=1, stride_axis=0)
        m_ref[0, p * SSM_C:(p + 1) * SSM_C, :] = jnp.where(causal, shifted, 0.0).astype(BF16)


def _ssm_prep(log_dt, a_row, a_col, b_t, c_t):
    g = log_dt.shape[0]
    return pl.pallas_call(
        _ssm_prep_kernel,
        grid=(g,),
        in_specs=[
            pl.BlockSpec((1, 1, 1), lambda i: (i, 0, 0)),
            pl.BlockSpec((1, 2, 2 * SSM_N), lambda i: (i, 0, 0)),
            pl.BlockSpec((1, SSM_N, 2), lambda i: (i, 0, 0)),
            pl.BlockSpec((1, 2, SSM_P, 2 * SSM_N), lambda i: (i, 0, 0, 0)),
            pl.BlockSpec((1, 2, SSM_N, SSM_P), lambda i: (i, 0, 0, 0)),
        ],
        out_specs=[
            pl.BlockSpec((1, SSM_K, SSM_K), lambda i: (i, 0, 0)),
            pl.BlockSpec((1, SSM_K, 2 * SSM_N), lambda i: (i, 0, 0)),
            pl.BlockSpec((1, 2 * SSM_N, SSM_K), lambda i: (i, 0, 0)),
            pl.BlockSpec((1, 2, 2 * SSM_N), lambda i: (i, 0, 0)),
        ],
        out_shape=[
            jax.ShapeDtypeStruct((g, SSM_K, SSM_K), BF16),
            jax.ShapeDtypeStruct((g, SSM_K, 2 * SSM_N), BF16),
            jax.ShapeDtypeStruct((g, 2 * SSM_N, SSM_K), BF16),
            jax.ShapeDtypeStruct((g, 2, 2 * SSM_N), F32),
        ],
        compiler_params=_cparams(1),
        name="ssm_prep",
    )(log_dt, a_row, a_col, b_t, c_t)


def _ssm_scan_kernel(u_ref, m_ref, w_ref, v_ref, ac_ref, d_ref, y_ref, z_scr, zs_scr, sp_scr,
                     *, bsz, n_chunks):
    pairs = n_chunks // 2
    n_pairs = bsz * pairs
    ut = u_ref[...]
    u2 = ut.reshape(SSM_P, n_pairs, LANES).transpose(1, 0, 2).reshape(n_pairs, SSM_P * LANES)
    low = lax.broadcasted_iota(jnp.int32, (n_pairs, LANES), 1) < SSM_C

    def col(a, j):
        return a[:, j * LANES:(j + 1) * LANES]

    def swap(a):
        return pltpu.roll(a, SSM_C, 1)

    even = [jnp.where(low, col(u2, 2 * k), swap(col(u2, 2 * k + 1))) for k in range(SSM_P // 2)]
    odd = [jnp.where(low, swap(col(u2, 2 * k)), col(u2, 2 * k + 1)) for k in range(SSM_P // 2)]
    ub = jnp.concatenate([jnp.concatenate(even, axis=1), jnp.concatenate(odd, axis=1)],
                         axis=0).astype(BF16)
    y = _dot(ub, m_ref[0])
    z = _dot(ub, w_ref[0])
    z_scr[...] = z
    zs_scr[...] = pltpu.roll(z, SSM_N, 1)

    a1 = ac_ref[0, 0:1, :]
    a2 = ac_ref[0, 1:2, :]
    s = jnp.zeros((bsz, 2 * SSM_N), F32)
    ss = jnp.zeros((bsz, 2 * SSM_N), F32)
    for c in range(n_chunks):
        rows = pl.ds((c % 2) * n_pairs + c // 2, bsz, stride=pairs)
        sp_scr[rows, :] = s
        if c + 1 < n_chunks:
            s, ss = (a1 * s + a2 * ss + z_scr[rows, :],
                     a1 * ss - a2 * s + zs_scr[rows, :])
    y = y + _dot(sp_scr[...].astype(BF16), v_ref[0])
    y_even, y_odd = y[:n_pairs], y[n_pairs:]
    cols = []
    for k in range(SSM_P // 2):
        ce, co = col(y_even, k), col(y_odd, k)
        cols += [jnp.where(low, ce, swap(co)), jnp.where(low, swap(ce), co)]
    y2 = jnp.concatenate(cols, axis=1)
    yt = y2.reshape(n_pairs, SSM_P, LANES).transpose(1, 0, 2).reshape(SSM_P, n_pairs * LANES)
    y_ref[...] = yt + d_ref[...] * ut


def _ssm_scan(zt, m, w, v, ac, d_col, *, bsz):
    n_tok = zt.shape[1]
    n_rows = n_tok // SSM_C
    n_chunks = n_rows // bsz
    kern = functools.partial(_ssm_scan_kernel, bsz=bsz, n_chunks=n_chunks)
    return pl.pallas_call(
        kern,
        grid=(SSM_G,),
        in_specs=[
            pl.BlockSpec((SSM_P, n_tok), lambda i: (i, 0)),
            pl.BlockSpec((1, SSM_K, SSM_K), lambda i: (i, 0, 0)),
            pl.BlockSpec((1, SSM_K, 2 * SSM_N), lambda i: (i, 0, 0)),
            pl.BlockSpec((1, 2 * SSM_N, SSM_K), lambda i: (i, 0, 0)),
            pl.BlockSpec((1, 2, 2 * SSM_N), lambda i: (i, 0, 0)),
            pl.BlockSpec((SSM_P, 1), lambda i: (i, 0)),
        ],
        out_specs=pl.BlockSpec((SSM_P, n_tok), lambda i: (i, 0)),
        out_shape=jax.ShapeDtypeStruct((TOK_WIDTH, n_tok), F32),
        scratch_shapes=[pltpu.VMEM((n_rows, 2 * SSM_N), F32)] * 3,
        compiler_params=_cparams(1),
        name="ssm_scan",
    )(zt, m, w, v, ac, d_col)


ATTN_BLOCK = 256


def _shared_kv_kernel(x_ref, g_ref, wk_ref, wvt_ref, kg_ref, k_ref, vt_ref):
    h = _rms(x_ref[...], g_ref[...]).astype(BF16)
    k_ref[...] = _seg_rms64(_dot(h, wk_ref[...]), kg_ref[...]).astype(BF16)
    vt_ref[0] = _dot_nt(wvt_ref[...], h).astype(BF16)


def _shared_kv(x, g, wk, wv_t, k_gain):
    n_tok = x.shape[0]
    tm = ATTN_BLOCK
    return pl.pallas_call(
        _shared_kv_kernel,
        grid=(n_tok // tm,),
        in_specs=[
            pl.BlockSpec((tm, D_MODEL), lambda i: (i, 0)),
            _const_spec((1, D_MODEL)),
            _const_spec(wk.shape),
            _const_spec(wv_t.shape),
            _const_spec((1, TOK_WIDTH)),
        ],
        out_specs=[pl.BlockSpec((tm, TOK_WIDTH), lambda i: (i, 0)),
                   pl.BlockSpec((1, TOK_WIDTH, tm), lambda i: (i, 0, 0))],
        out_shape=[jax.ShapeDtypeStruct((n_tok, TOK_WIDTH), BF16),
                   jax.ShapeDtypeStruct((n_tok // tm, TOK_WIDTH, tm), BF16)],
        compiler_params=_cparams(1),
        name="shared_kv",
    )(x, g, wk, wv_t, k_gain)


N_STREAMS = 2 * DIFF_HEADS


def _diff_attn_kernel(qt_ref, k_ref, vt_ref, lam_ref, subg_ref, o_ref, qm_scr, sc_scr,
                      bm_scr, m_scr, acc_scr, *, lambda_init):
    tq = ATTN_BLOCK
    qi = pl.program_id(1)
    lp = lam_ref[...]
    lam = (jnp.exp(jnp.sum(lp[0:1] * lp[1:2], axis=-1, keepdims=True))
           - jnp.exp(jnp.sum(lp[2:3] * lp[3:4], axis=-1, keepdims=True)) + lambda_init)
    key_chunk = lax.broadcasted_iota(jnp.int32, (tq, tq), 0) // CHUNK
    qry_chunk = lax.broadcasted_iota(jnp.int32, (tq, tq), 1) // CHUNK
    diag_mask = key_chunk <= qry_chunk
    first_half = lax.broadcasted_iota(jnp.int32, (LANES, tq), 0) < DIFF_HEAD_DIM
    n_grp = DIFF_HEADS // 2

    for kg in range(2 * n_grp):
        qg = qt_ref[kg * LANES:(kg + 1) * LANES, :]
        zero = jnp.zeros_like(qg)
        qm_scr[kg, :, 0:tq] = jnp.where(first_half, qg, zero)
        qm_scr[kg, :, tq:2 * tq] = jnp.where(first_half, zero, qg)

    def block(j, diagonal):
        rows = pl.ds(pl.multiple_of(j * tq, tq), tq)
        for kg in range(2 * n_grp):
            kb = k_ref[rows, kg * LANES:(kg + 1) * LANES]
            st = _dot(kb, qm_scr[kg])
            if diagonal:
                st = jnp.where(jnp.concatenate([diag_mask, diag_mask], axis=1), st, -jnp.inf)
            sc_scr[kg] = st
            bm_scr[kg] = jnp.max(st, axis=0, keepdims=True)
        for kg in range(2 * n_grp):
            g = kg % n_grp
            if diagonal:
                m_new = bm_scr[kg]
            else:
                m_old = m_scr[kg]
                m_new = jnp.maximum(m_old, bm_scr[kg])
                alpha = jnp.exp2(m_old - m_new)
            pb = jnp.exp2(sc_scr[kg] - m_new).astype(BF16)
            m_scr[kg] = m_new
            pv = jnp.concatenate(
                [_dot(_with_ones_rows(vt_ref[j, h * DIFF_V_DIM:(h + 1) * DIFF_V_DIM, :]),
                      pb[:, hh * tq:(hh + 1) * tq])
                 for hh, h in enumerate((2 * g, 2 * g + 1))], axis=1)
            acc_scr[kg] = pv if diagonal else alpha * acc_scr[kg] + pv

    def body(j, carry):
        block(j, False)
        return carry

    block(qi, True)
    lax.fori_loop(0, qi, body, 0)

    for head in range(DIFF_HEADS):
        rows = slice(head * DIFF_V_DIM, (head + 1) * DIFF_V_DIM)
        q_cols = slice((head % 2) * tq, (head % 2 + 1) * tq)
        on = []
        for comp in range(2):
            acc = acc_scr[comp * n_grp + head // 2]
            on.append(acc[0:DIFF_V_DIM, q_cols] * (1.0 / acc[DIFF_V_DIM:DIFF_V_DIM + 1, q_cols]))
        o = on[0] - lam * on[1]
        o = o * lax.rsqrt(jnp.mean(o * o, axis=0, keepdims=True) + EPS)
        o_ref[rows, :] = (o * subg_ref[rows, :] * (1.0 - lambda_init)).astype(o_ref.dtype)


def _diff_attn(qt, k, vt, lam_params, sub_gain, *, bsz, seq, lambda_init):
    tq = ATTN_BLOCK
    n_q = seq // tq
    return pl.pallas_call(
        functools.partial(_diff_attn_kernel, lambda_init=lambda_init),
        grid=(bsz, n_q),
        in_specs=[
            pl.BlockSpec((TOK_WIDTH, tq), lambda b, i: (0, b * n_q + i)),
            pl.BlockSpec((seq, TOK_WIDTH), lambda b, i: (b, 0)),
            pl.BlockSpec((n_q, TOK_WIDTH, tq), lambda b, i: (b, 0, 0)),
            pl.BlockSpec((4, DIFF_HEAD_DIM), lambda b, i: (0, 0)),
            pl.BlockSpec((TOK_WIDTH, 1), lambda b, i: (0, 0)),
        ],
        out_specs=pl.BlockSpec((TOK_WIDTH, tq), lambda b, i: (0, b * n_q + i)),
        out_shape=jax.ShapeDtypeStruct((TOK_WIDTH, bsz * seq), F32),
        scratch_shapes=[pltpu.VMEM((DIFF_HEADS, LANES, 2 * tq), BF16),
                        pltpu.VMEM((DIFF_HEADS, tq, 2 * tq), F32),
                        pltpu.VMEM((DIFF_HEADS, 1, 2 * tq), F32),
                        pltpu.VMEM((DIFF_HEADS, 1, 2 * tq), F32),
                        pltpu.VMEM((DIFF_HEADS, DIFF_V_DIM + ONES_ROWS, 2 * tq), F32)],
        compiler_params=_cparams(2),
        name="diff_attn",
    )(qt, k, vt, lam_params, sub_gain)


def _gelu_tanh(x):
    return 0.5 * x * (1.0 + jnp.tanh(math.sqrt(2.0 / math.pi) * (x + 0.044715 * (x * x * x))))


def _sigmoid(x):
    return 1.0 / (1.0 + jnp.exp(-x))


def _post_kernel(tok_ref, omem_ref, x_ref, wglu_ref, wout_ref, g_ref, wg_ref, wu_ref,
                 wd_ref, o_ref, *, s5):
    tok = tok_ref[...].T
    if s5:
        h = _gelu_tanh(tok)
        gate = _sigmoid(_dot(h.astype(BF16), wglu_ref[...]))
        tok = h * gate
    mix = jnp.concatenate([tok.astype(BF16), omem_ref[...].T.astype(BF16)], axis=1)
    x1 = x_ref[...] + _dot(mix, wout_ref[...])
    h2 = _rms(x1, g_ref[...]).astype(BF16)
    gate = _dot(h2, wg_ref[...])
    up = _dot(h2, wu_ref[...])
    act = (gate * _sigmoid(gate) * up).astype(BF16)
    o_ref[...] = x1 + _dot(act, wd_ref[...])


def _post(tok, o_mem, x, w_glu, w_out, g, wg, wu, wd, *, s5, tm=512):
    n_tok = x.shape[0]
    return pl.pallas_call(
        functools.partial(_post_kernel, s5=s5),
        grid=(n_tok // tm,),
        in_specs=[
            pl.BlockSpec((TOK_WIDTH, tm), lambda i: (0, i)),
            pl.BlockSpec((MEM_WIDTH, tm), lambda i: (0, i)),
            pl.BlockSpec((tm, D_MODEL), lambda i: (i, 0)),
            _const_spec(w_glu.shape),
            _const_spec(w_out.shape),
            _const_spec((1, D_MODEL)),
            _const_spec(wg.shape),
            _const_spec(wu.shape),
            _const_spec(wd.shape),
        ],
        out_specs=pl.BlockSpec((tm, D_MODEL), lambda i: (i, 0)),
        out_shape=jax.ShapeDtypeStruct((n_tok, D_MODEL), F32),
        compiler_params=_cparams(1),
        name="post_s5" if s5 else "post",
    )(tok, o_mem, x, w_glu, w_out, g, wg, wu, wd)


def _tile_gain(g, reps):
    return jnp.tile(g.astype(F32), reps)[None, :]


def kernel(x, mem, norm_mix, norm_ffn, norm_mem, w_in, w_out, mem_wk, mem_wv, mem_q_norm,
           mem_k_norm, ffn_w_gate, ffn_w_up, ffn_w_down, ssm_a_re, ssm_a_im, ssm_log_dt,
           ssm_b_re, ssm_b_im, ssm_c_re, ssm_c_im, ssm_d, ssm_w_glu, kv_norm, diff_wk, diff_wv,
           diff_k_norm, diff_q_norm, diff_lambda_q1, diff_lambda_k1, diff_lambda_q2,
           diff_lambda_k2, diff_sub_norm):
    bsz, seq, _ = x.shape
    n_tok = bsz * seq
    xf = x.reshape(n_tok, D_MODEL)

    k_gain = jnp.tile(mem_k_norm.astype(F32), (1, MEM_HEADS))[:, None, :]
    k_mem, vt_mem = _mem_kv(mem, norm_mem[:, None, :], mem_wk.astype(BF16),
                            mem_wv.astype(BF16).transpose(0, 2, 1), k_gain)

    k_sh = v_sh = None
    for layer in range(DEPTH):
        s5 = layer < N_A_LAYERS
        q_gain = _tile_gain(mem_q_norm[layer], MEM_HEADS).T
        if s5:
            tok_gain = jnp.ones((TOK_WIDTH, 1), F32)
        else:
            tok_gain = _tile_gain(diff_q_norm[layer - N_A_LAYERS], 2 * DIFF_HEADS).T
        z_tok, o_mem = _in_proj(xf, norm_mix[layer][None, :], w_in[layer].astype(BF16).T,
                                tok_gain, q_gain, k_mem[layer], vt_mem[layer],
                                seq=seq, qk_norm=not s5)
        if s5:
            i = layer
            log_dt = ssm_log_dt[i][:, None, None]
            a_row = jnp.tile(jnp.stack([ssm_a_re[i], ssm_a_im[i]], axis=1), (1, 1, 2))
            a_col = jnp.stack([ssm_a_re[i], ssm_a_im[i]], axis=2)
            b_t = jnp.tile(jnp.stack([ssm_b_re[i], ssm_b_im[i]], axis=1).transpose(0, 1, 3, 2),
                           (1, 1, 1, 2))
            c_t = jnp.stack([ssm_c_re[i], ssm_c_im[i]], axis=1).transpose(0, 1, 3, 2)
            m_op, w_op, v_op, ac = _ssm_prep(log_dt, a_row, a_col, b_t, c_t)
            tok = _ssm_scan(z_tok, m_op, w_op, v_op, ac, ssm_d[i][:, None], bsz=bsz)
            w_glu = ssm_w_glu[i].astype(BF16)
        else:
            j = layer - N_A_LAYERS
            lambda_init = 0.8 - 0.6 * math.exp(-0.3 * layer)
            lam_params = jnp.stack([diff_lambda_q1[j], diff_lambda_k1[j],
                                    diff_lambda_q2[j], diff_lambda_k2[j]]).astype(F32)
            tok = _diff_attn(z_tok, k_sh, v_sh, lam_params,
                             _tile_gain(diff_sub_norm[j], DIFF_HEADS).T,
                             bsz=bsz, seq=seq, lambda_init=lambda_init)
            w_glu = jnp.zeros((8, LANES), BF16)
        xf = _post(tok, o_mem, xf, w_glu, w_out[layer].astype(BF16),
                   norm_ffn[layer][None, :], ffn_w_gate[layer].astype(BF16),
                   ffn_w_up[layer].astype(BF16), ffn_w_down[layer].astype(BF16), s5=s5)
        if layer == N_A_LAYERS - 1:
            k_sh, v_sh = _shared_kv(xf, kv_norm[None, :], diff_wk.astype(BF16),
                                    diff_wv.astype(BF16).T,
                                    _tile_gain(diff_k_norm, 2 * DIFF_HEADS))
    return xf.reshape(bsz, seq, D_MODEL)
```

```python
import functools
import math

import jax
import jax.numpy as jnp
from jax import lax
from jax.experimental import pallas as pl
from jax.experimental.pallas import tpu as pltpu

D_MODEL = 1024
DEPTH = 4
CHUNK = 64
N_A_LAYERS = DEPTH // 2
MEM_HEADS = 4
MEM_HEAD_DIM = 64
MEM_WIDTH = MEM_HEADS * MEM_HEAD_DIM
TOK_WIDTH = D_MODEL - MEM_WIDTH
SSM_P = 16
SSM_G = TOK_WIDTH // SSM_P
SSM_N = 64
SSM_C = 64
SSM_K = SSM_P * SSM_C
DIFF_HEAD_DIM = 64
DIFF_HEADS = TOK_WIDTH // (2 * DIFF_HEAD_DIM)
DIFF_V_DIM = 2 * DIFF_HEAD_DIM
D_FF = 2816
EPS = 1e-6

LANES = 128
VMEM_LIMIT = 56 * 1024 * 1024

F32 = jnp.float32
BF16 = jnp.bfloat16


def _cparams(n_grid, vmem=VMEM_LIMIT, flags=None):
    return pltpu.CompilerParams(
        dimension_semantics=("arbitrary",) * n_grid, vmem_limit_bytes=vmem, flags=flags)


def _const_spec(shape):
    nd = len(shape)
    return pl.BlockSpec(shape, lambda *_: (0,) * nd, pipeline_mode=pl.Buffered(1))


def _rms(x, g):
    return x * lax.rsqrt(jnp.mean(x * x, axis=-1, keepdims=True) + EPS) * g


def _seg_rms64(x, g):
    rows, width = x.shape
    lane = lax.broadcasted_iota(jnp.int32, (rows, LANES), 1)
    lo = lane < 64
    outs = []
    for v in range(width // LANES):
        xv = x[:, v * LANES:(v + 1) * LANES]
        sq = xv * xv
        s_lo = jnp.sum(jnp.where(lo, sq, 0.0), axis=-1, keepdims=True)
        s_hi = jnp.sum(jnp.where(lo, 0.0, sq), axis=-1, keepdims=True)
        r_lo = lax.rsqrt(s_lo * (1.0 / 64) + EPS)
        r_hi = lax.rsqrt(s_hi * (1.0 / 64) + EPS)
        outs.append(xv * jnp.where(lo, r_lo, r_hi) * g[:, v * LANES:(v + 1) * LANES])
    return jnp.concatenate(outs, axis=1)


def _half_masks(rows, dtype):
    lane = lax.broadcasted_iota(jnp.int32, (rows, LANES), 1)
    lo = (lane < 64).astype(dtype)
    return lo, (1 - lo).astype(dtype)


def _dot(a, b):
    return jnp.dot(a, b, preferred_element_type=F32)


def _dot_nt(a, b):
    return lax.dot_general(a, b, (((1,), (1,)), ((), ())), preferred_element_type=F32)


def _mem_kv_kernel(mem_ref, g_ref, wk_ref, wv_ref, kg_ref, k_ref, vt_ref):
    mn = _rms(mem_ref[0], g_ref[0]).astype(BF16)
    k_ref[0, 0] = _seg_rms64(_dot(mn, wk_ref[0].astype(BF16)), kg_ref[0]).astype(BF16)
    vt_ref[0, 0] = _dot(mn, wv_ref[0].astype(BF16)).T.astype(BF16)


def _mem_kv(mem, norm_mem, wk, wv, k_gain):
    bsz, n_mem, _ = mem.shape
    return pl.pallas_call(
        _mem_kv_kernel,
        grid=(DEPTH, bsz),
        in_specs=[
            pl.BlockSpec((1, n_mem, D_MODEL), lambda l, b: (b, 0, 0)),
            pl.BlockSpec((1, 1, D_MODEL), lambda l, b: (l, 0, 0)),
            pl.BlockSpec((1, D_MODEL, MEM_WIDTH), lambda l, b: (l, 0, 0)),
            pl.BlockSpec((1, D_MODEL, MEM_WIDTH), lambda l, b: (l, 0, 0)),
            pl.BlockSpec((1, 1, MEM_WIDTH), lambda l, b: (l, 0, 0)),
        ],
        out_specs=[pl.BlockSpec((1, 1, n_mem, MEM_WIDTH), lambda l, b: (l, b, 0, 0)),
                   pl.BlockSpec((1, 1, MEM_WIDTH, n_mem), lambda l, b: (l, b, 0, 0))],
        out_shape=[jax.ShapeDtypeStruct((DEPTH, bsz, n_mem, MEM_WIDTH), BF16),
                   jax.ShapeDtypeStruct((DEPTH, bsz, MEM_WIDTH, n_mem), BF16)],
        compiler_params=_cparams(2),
        name="mem_kv",
    )(mem, norm_mem, wk, wv, k_gain)


def _seg_rms_rows(xt, g_col, seg):
    width, cols = xt.shape
    x3 = xt.reshape(width // seg, seg, cols)
    ms = jnp.mean(x3 * x3, axis=1, keepdims=True)
    return (x3 * lax.rsqrt(ms + EPS)).reshape(width, cols) * g_col


ONES_ROWS = 16
SOFTMAX_LOG2E = math.log2(math.e)


def _with_ones_rows(vt):
    return jnp.concatenate([vt, jnp.ones((ONES_ROWS, vt.shape[1]), vt.dtype)], axis=0)


def _mem_attention_t(qt, k, vt):
    tm = qt.shape[1]
    first = lax.broadcasted_iota(jnp.int32, (LANES, tm), 0) < MEM_HEAD_DIM
    outs = []
    for grp in range(MEM_WIDTH // LANES):
        rows = slice(grp * LANES, (grp + 1) * LANES)
        qg = qt[rows, :]
        zero = jnp.zeros_like(qg)
        qm = jnp.concatenate([jnp.where(first, qg, zero), jnp.where(first, zero, qg)], axis=1)
        s = _dot(k[:, rows], qm)
        p = jnp.exp2(s - jnp.max(s, axis=0, keepdims=True))
        o2 = _dot(_with_ones_rows(vt[rows, :]), p.astype(BF16))
        o2 = o2[:LANES] * (1.0 / o2[LANES:LANES + 1])
        outs.append(jnp.where(first, o2[:, :tm], o2[:, tm:]))
    return jnp.concatenate(outs, axis=0)


def _in_proj_kernel(x_ref, g_ref, win_ref, tokg_ref, qg_ref, k_ref, vt_ref,
                    ztok_ref, omem_ref, wt_scr, *, qk_norm):
    @pl.when(pl.program_id(0) == 0)
    def _():
        wt_scr[...] = win_ref[...].T.astype(BF16)

    h = _rms(x_ref[...], g_ref[...]).astype(BF16)
    zt = _dot_nt(wt_scr[...], h)
    z_tok = zt[:TOK_WIDTH]
    if qk_norm:
        z_tok = (_seg_rms_rows(z_tok, tokg_ref[...], DIFF_HEAD_DIM)
                 * (DIFF_HEAD_DIM ** -0.5 * SOFTMAX_LOG2E))
    ztok_ref[...] = z_tok.astype(ztok_ref.dtype)
    q_mem = (_seg_rms_rows(zt[TOK_WIDTH:], qg_ref[...], MEM_HEAD_DIM)
             * (MEM_HEAD_DIM ** -0.5 * SOFTMAX_LOG2E))
    omem_ref[...] = _mem_attention_t(q_mem.astype(BF16), k_ref[0], vt_ref[0]).astype(omem_ref.dtype)


def _in_proj(x, g, w_in, tok_gain, q_gain, k_mem, vt_mem, *, seq, qk_norm, tm=512):
    n_tok = x.shape[0]
    n_mem = k_mem.shape[1]
    tiles_per_seq = seq // tm
    return pl.pallas_call(
        functools.partial(_in_proj_kernel, qk_norm=qk_norm),
        grid=(n_tok // tm,),
        in_specs=[
            pl.BlockSpec((tm, D_MODEL), lambda i: (i, 0)),
            _const_spec((1, D_MODEL)),
            _const_spec(w_in.shape),
            _const_spec((TOK_WIDTH, 1)),
            _const_spec((MEM_WIDTH, 1)),
            pl.BlockSpec((1, n_mem, MEM_WIDTH), lambda i: (i // tiles_per_seq, 0, 0)),
            pl.BlockSpec((1, MEM_WIDTH, n_mem), lambda i: (i // tiles_per_seq, 0, 0)),
        ],
        out_specs=[pl.BlockSpec((TOK_WIDTH, tm), lambda i: (0, i)),
                   pl.BlockSpec((MEM_WIDTH, tm), lambda i: (0, i))],
        out_shape=[jax.ShapeDtypeStruct((TOK_WIDTH, n_tok), BF16 if qk_norm else F32),
                   jax.ShapeDtypeStruct((MEM_WIDTH, n_tok), F32)],
        scratch_shapes=[pltpu.VMEM((D_MODEL, D_MODEL), BF16)],
        compiler_params=_cparams(1),
        name="in_proj_q" if qk_norm else "in_proj_u",
    )(x, g, w_in, tok_gain, q_gain, k_mem, vt_mem)


def _cpow(ar, ai, dt, k):
    mag = jnp.exp(k * dt * ar)
    ang = k * dt * ai
    return mag * jnp.cos(ang), mag * jnp.sin(ang)


def _zoh_coef(ar, ai, abr, abi):
    den = ar * ar + ai * ai
    nr = abr - 1.0
    return (nr * ar + abi * ai) / den, (abi * ar - nr * ai) / den


def _ssm_prep_kernel(ldt_ref, arow_ref, acol_ref, bt_ref, ct_ref, m_ref, w_ref, v_ref, ac_ref):
    dt = jnp.exp(ldt_ref[0])
    first = lax.broadcasted_iota(jnp.int32, (1, LANES), 1) < SSM_N
    ar_r, ai_r = arow_ref[0, 0:1, :], arow_ref[0, 1:2, :]
    ar_c, ai_c = acol_ref[0, :, 0:1], acol_ref[0, :, 1:2]

    s_row = lax.broadcasted_iota(jnp.int32, (SSM_C, LANES), 0).astype(F32)
    e_re, e_im = _cpow(ar_r, ai_r, dt, float(SSM_C - 1) - s_row)
    ab_re, ab_im = e_re[SSM_C - 2:SSM_C - 1], e_im[SSM_C - 2:SSM_C - 1]
    cf_re, cf_im = _zoh_coef(ar_r, ai_r, ab_re, ab_im)

    pc_re = e_re[0:1] * ab_re - e_im[0:1] * ab_im
    pc_im = e_re[0:1] * ab_im + e_im[0:1] * ab_re
    ac_ref[0, 0:1, :] = pc_re
    ac_ref[0, 1:2, :] = jnp.where(first, -pc_im, pc_im)

    bre, bim = bt_ref[0, 0], bt_ref[0, 1]
    bb_re = cf_re * bre - cf_im * bim
    bb_im = cf_re * bim + cf_im * bre

    for p in range(SSM_P):
        br, bi = bb_re[p:p + 1, :], bb_im[p:p + 1, :]
        w_ref[0, p * SSM_C:(p + 1) * SSM_C, :] = jnp.where(
            first, br * e_re - bi * e_im, br * e_im + bi * e_re).astype(BF16)

    lane_n = lax.broadcasted_iota(jnp.int32, (SSM_N, LANES), 1)
    tau = (lane_n % SSM_C).astype(F32)
    p0_re, p0_im = _cpow(ar_c, ai_c, dt, tau)
    ab_re_c, ab_im_c = p0_re[:, 1:2], p0_im[:, 1:2]
    p1_re = p0_re * ab_re_c - p0_im * ab_im_c
    p1_im = p0_re * ab_im_c + p0_im * ab_re_c
    reps = SSM_K // LANES
    p0_re, p0_im = jnp.concatenate([p0_re] * reps, 1), jnp.concatenate([p0_im] * reps, 1)
    p1_re, p1_im = jnp.concatenate([p1_re] * reps, 1), jnp.concatenate([p1_im] * reps, 1)

    sel = (lax.broadcasted_iota(jnp.int32, (SSM_P, SSM_K), 1) // SSM_C
           == lax.broadcasted_iota(jnp.int32, (SSM_P, SSM_K), 0)).astype(F32)

    def expand(ct):
        return jnp.dot(ct, sel, precision=lax.Precision.HIGHEST, preferred_element_type=F32)

    c_re, c_im = expand(ct_ref[0, 0]), expand(ct_ref[0, 1])

    v_ref[0, 0:SSM_N, :] = (c_re * p1_re - c_im * p1_im).astype(BF16)
    v_ref[0, SSM_N:2 * SSM_N, :] = (-(c_re * p1_im + c_im * p1_re)).astype(BF16)

    ce = jnp.concatenate([c_re * p0_re - c_im * p0_im, -(c_re * p0_im + c_im * p0_re)], axis=0)
    kt = jnp.dot(jnp.where(first, bb_re, bb_im), ce, precision=lax.Precision.HIGHEST,
                 preferred_element_type=F32)

    row = lax.broadcasted_iota(jnp.int32, (SSM_C, SSM_K), 0)
    lane_t = lax.broadcasted_iota(jnp.int32, (SSM_C, SSM_K), 1) % SSM_C
    causal = lane_t >= row
    for p in range(SSM_P):
        base = jnp.broadcast_to(kt[p:p + 1, :], (SSM_C, SSM_K))
        shifted = pltpu.roll(base, 0, 1, stride=1, stride_axis=0)
        m_ref[0, p * SSM_C:(p + 1) * SSM_C, :] = jnp.where(causal, shifted, 0.0).astype(BF16)


def _ssm_prep(log_dt, a_row, a_col, b_t, c_t):
    g = log_dt.shape[0]
    return pl.pallas_call(
        _ssm_prep_kernel,
        grid=(g,),
        in_specs=[
            pl.BlockSpec((1, 1, 1), lambda i: (i, 0, 0)),
            pl.BlockSpec((1, 2, 2 * SSM_N), lambda i: (i, 0, 0)),
            pl.BlockSpec((1, SSM_N, 2), lambda i: (i, 0, 0)),
            pl.BlockSpec((1, 2, SSM_P, 2 * SSM_N), lambda i: (i, 0, 0, 0)),
            pl.BlockSpec((1, 2, SSM_N, SSM_P), lambda i: (i, 0, 0, 0)),
        ],
        out_specs=[
            pl.BlockSpec((1, SSM_K, SSM_K), lambda i: (i, 0, 0)),
            pl.BlockSpec((1, SSM_K, 2 * SSM_N), lambda i: (i, 0, 0)),
            pl.BlockSpec((1, 2 * SSM_N, SSM_K), lambda i: (i, 0, 0)),
            pl.BlockSpec((1, 2, 2 * SSM_N), lambda i: (i, 0, 0)),
        ],
        out_shape=[
            jax.ShapeDtypeStruct((g, SSM_K, SSM_K), BF16),
            jax.ShapeDtypeStruct((g, SSM_K, 2 * SSM_N), BF16),
            jax.ShapeDtypeStruct((g, 2 * SSM_N, SSM_K), BF16),
            jax.ShapeDtypeStruct((g, 2, 2 * SSM_N), F32),
        ],
        compiler_params=_cparams(1),
        name="ssm_prep",
    )(log_dt, a_row, a_col, b_t, c_t)


def _ssm_scan_kernel(u_ref, m_ref, w_ref, v_ref, ac_ref, d_ref, y_ref, z_scr, zs_scr, sp_scr,
                     *, bsz, n_chunks):
    pairs = n_chunks // 2
    n_pairs = bsz * pairs
    ut = u_ref[...]
    u2 = ut.reshape(SSM_P, n_pairs, LANES).transpose(1, 0, 2).reshape(n_pairs, SSM_P * LANES)
    low = lax.broadcasted_iota(jnp.int32, (n_pairs, LANES), 1) < SSM_C

    def col(a, j):
        return a[:, j * LANES:(j + 1) * LANES]

    def swap(a):
        return pltpu.roll(a, SSM_C, 1)

    even = [jnp.where(low, col(u2, 2 * k), swap(col(u2, 2 * k + 1))) for k in range(SSM_P // 2)]
    odd = [jnp.where(low, swap(col(u2, 2 * k)), col(u2, 2 * k + 1)) for k in range(SSM_P // 2)]
    ub = jnp.concatenate([jnp.concatenate(even, axis=1), jnp.concatenate(odd, axis=1)],
                         axis=0).astype(BF16)
    y = _dot(ub, m_ref[0])
    z = _dot(ub, w_ref[0])
    z_scr[...] = z
    zs_scr[...] = pltpu.roll(z, SSM_N, 1)

    a1 = ac_ref[0, 0:1, :]
    a2 = ac_ref[0, 1:2, :]
    s = jnp.zeros((bsz, 2 * SSM_N), F32)
    ss = jnp.zeros((bsz, 2 * SSM_N), F32)
    for c in range(n_chunks):
        rows = pl.ds((c % 2) * n_pairs + c // 2, bsz, stride=pairs)
        sp_scr[rows, :] = s
        if c + 1 < n_chunks:
            s, ss = (a1 * s + a2 * ss + z_scr[rows, :],
                     a1 * ss - a2 * s + zs_scr[rows, :])
    y = y + _dot(sp_scr[...].astype(BF16), v_ref[0])
    y_even, y_odd = y[:n_pairs], y[n_pairs:]
    cols = []
    for k in range(SSM_P // 2):
        ce, co = col(y_even, k), col(y_odd, k)
        cols += [jnp.where(low, ce, swap(co)), jnp.where(low, swap(ce), co)]
    y2 = jnp.concatenate(cols, axis=1)
    yt = y2.reshape(n_pairs, SSM_P, LANES).transpose(1, 0, 2).reshape(SSM_P, n_pairs * LANES)
    y_ref[...] = yt + d_ref[...] * ut


def _ssm_scan(zt, m, w, v, ac, d_col, *, bsz):
    n_tok = zt.shape[1]
    n_rows = n_tok // SSM_C
    n_chunks = n_rows // bsz
    kern = functools.partial(_ssm_scan_kernel, bsz=bsz, n_chunks=n_chunks)
    return pl.pallas_call(
        kern,
        grid=(SSM_G,),
        in_specs=[
            pl.BlockSpec((SSM_P, n_tok), lambda i: (i, 0)),
            pl.BlockSpec((1, SSM_K, SSM_K), lambda i: (i, 0, 0)),
            pl.BlockSpec((1, SSM_K, 2 * SSM_N), lambda i: (i, 0, 0)),
            pl.BlockSpec((1, 2 * SSM_N, SSM_K), lambda i: (i, 0, 0)),
            pl.BlockSpec((1, 2, 2 * SSM_N), lambda i: (i, 0, 0)),
            pl.BlockSpec((SSM_P, 1), lambda i: (i, 0)),
        ],
        out_specs=pl.BlockSpec((SSM_P, n_tok), lambda i: (i, 0)),
        out_shape=jax.ShapeDtypeStruct((TOK_WIDTH, n_tok), F32),
        scratch_shapes=[pltpu.VMEM((n_rows, 2 * SSM_N), F32)] * 3,
        compiler_params=_cparams(1),
        name="ssm_scan",
    )(zt, m, w, v, ac, d_col)


ATTN_BLOCK = 256
KV_SUB = 2


def _shared_kv_kernel(x_ref, g_ref, wk_ref, wv_ref, kg_ref, k_ref, vt_ref):
    h = _rms(x_ref[...], g_ref[...]).astype(BF16)
    k_ref[...] = _seg_rms64(_dot(h, wk_ref[...].astype(BF16)), kg_ref[...]).astype(BF16)
    v = _dot(h, wv_ref[...].astype(BF16))
    for j in range(vt_ref.shape[0]):
        vt_ref[j] = v[j * ATTN_BLOCK:(j + 1) * ATTN_BLOCK].T.astype(BF16)


def _shared_kv(x, g, wk, wv, k_gain, tm=512):
    n_tok = x.shape[0]
    sub = tm // ATTN_BLOCK
    return pl.pallas_call(
        _shared_kv_kernel,
        grid=(n_tok // tm,),
        in_specs=[
            pl.BlockSpec((tm, D_MODEL), lambda i: (i, 0)),
            _const_spec((1, D_MODEL)),
            _const_spec(wk.shape),
            _const_spec(wv.shape),
            _const_spec((1, TOK_WIDTH)),
        ],
        out_specs=[pl.BlockSpec((tm, TOK_WIDTH), lambda i: (i, 0)),
                   pl.BlockSpec((sub, TOK_WIDTH, ATTN_BLOCK), lambda i: (i, 0, 0))],
        out_shape=[jax.ShapeDtypeStruct((n_tok, TOK_WIDTH), BF16),
                   jax.ShapeDtypeStruct((n_tok // ATTN_BLOCK, TOK_WIDTH, ATTN_BLOCK), BF16)],
        compiler_params=_cparams(1),
        name="shared_kv",
    )(x, g, wk, wv, k_gain)


N_STREAMS = 2 * DIFF_HEADS


def _diff_attn_kernel(qt_ref, k_ref, vt_ref, lam_ref, subg_ref, o_ref, qm_scr, sc_scr,
                      bm_scr, m_scr, acc_scr, *, lambda_init):
    tq = ATTN_BLOCK
    qi = pl.program_id(1)
    lp = lam_ref[...]
    lam = (jnp.exp(jnp.sum(lp[0:1] * lp[1:2], axis=-1, keepdims=True))
           - jnp.exp(jnp.sum(lp[2:3] * lp[3:4], axis=-1, keepdims=True)) + lambda_init)
    key_chunk = lax.broadcasted_iota(jnp.int32, (tq, tq), 0) // CHUNK
    qry_chunk = lax.broadcasted_iota(jnp.int32, (tq, tq), 1) // CHUNK
    diag_mask = key_chunk <= qry_chunk
    first_half = lax.broadcasted_iota(jnp.int32, (LANES, tq), 0) < DIFF_HEAD_DIM
    n_grp = DIFF_HEADS // 2

    for kg in range(2 * n_grp):
        qg = qt_ref[kg * LANES:(kg + 1) * LANES, :]
        zero = jnp.zeros_like(qg)
        qm_scr[kg, :, 0:tq] = jnp.where(first_half, qg, zero)
        qm_scr[kg, :, tq:2 * tq] = jnp.where(first_half, zero, qg)

    def block(j, diagonal, n_sub=1):
        nk = n_sub * tq
        rows = pl.ds(pl.multiple_of(j * tq, tq), nk)
        for kg in range(2 * n_grp):
            kb = k_ref[rows, kg * LANES:(kg + 1) * LANES]
            st = _dot(kb, qm_scr[kg])
            if diagonal:
                st = jnp.where(jnp.concatenate([diag_mask, diag_mask], axis=1), st, -jnp.inf)
            sc_scr[kg, 0:nk, :] = st
            bm_scr[kg] = jnp.max(st, axis=0, keepdims=True)
        for kg in range(2 * n_grp):
            g = kg % n_grp
            if diagonal:
                m_new = bm_scr[kg]
            else:
                m_old = m_scr[kg]
                m_new = jnp.maximum(m_old, bm_scr[kg])
                alpha = jnp.exp2(m_old - m_new)
            pb = jnp.exp2(sc_scr[kg, 0:nk, :] - m_new).astype(BF16)
            m_scr[kg] = m_new

            def v_rows(h):
                vt = [vt_ref[j + i, h * DIFF_V_DIM:(h + 1) * DIFF_V_DIM, :] for i in range(n_sub)]
                return _with_ones_rows(vt[0] if n_sub == 1 else jnp.concatenate(vt, axis=1))

            pv = jnp.concatenate(
                [_dot(v_rows(h), pb[:, hh * tq:(hh + 1) * tq])
                 for hh, h in enumerate((2 * g, 2 * g + 1))], axis=1)
            acc_scr[kg] = pv if diagonal else alpha * acc_scr[kg] + pv

    def body(t, carry):
        block(KV_SUB * t, False, KV_SUB)
        return carry

    block(qi, True)
    lax.fori_loop(0, qi // KV_SUB, body, 0)
    for r in range(1, KV_SUB):
        @pl.when(qi % KV_SUB >= r)
        def _():
            block(qi - qi % KV_SUB + r - 1, False)

    for head in range(DIFF_HEADS):
        rows = slice(head * DIFF_V_DIM, (head + 1) * DIFF_V_DIM)
        q_cols = slice((head % 2) * tq, (head % 2 + 1) * tq)
        on = []
        for comp in range(2):
            acc = acc_scr[comp * n_grp + head // 2]
            on.append(acc[0:DIFF_V_DIM, q_cols] * (1.0 / acc[DIFF_V_DIM:DIFF_V_DIM + 1, q_cols]))
        o = on[0] - lam * on[1]
        o = o * lax.rsqrt(jnp.mean(o * o, axis=0, keepdims=True) + EPS)
        o_ref[rows, :] = (o * subg_ref[rows, :] * (1.0 - lambda_init)).astype(o_ref.dtype)


def _diff_attn(qt, k, vt, lam_params, sub_gain, *, bsz, seq, lambda_init):
    tq = ATTN_BLOCK
    n_q = seq // tq
    return pl.pallas_call(
        functools.partial(_diff_attn_kernel, lambda_init=lambda_init),
        grid=(bsz, n_q),
        in_specs=[
            pl.BlockSpec((TOK_WIDTH, tq), lambda b, i: (0, b * n_q + i)),
            pl.BlockSpec((seq, TOK_WIDTH), lambda b, i: (b, 0)),
            pl.BlockSpec((n_q, TOK_WIDTH, tq), lambda b, i: (b, 0, 0)),
            pl.BlockSpec((4, DIFF_HEAD_DIM), lambda b, i: (0, 0)),
            pl.BlockSpec((TOK_WIDTH, 1), lambda b, i: (0, 0)),
        ],
        out_specs=pl.BlockSpec((TOK_WIDTH, tq), lambda b, i: (0, b * n_q + i)),
        out_shape=jax.ShapeDtypeStruct((TOK_WIDTH, bsz * seq), F32),
        scratch_shapes=[pltpu.VMEM((DIFF_HEADS, LANES, 2 * tq), BF16),
                        pltpu.VMEM((DIFF_HEADS, KV_SUB * tq, 2 * tq), F32),
                        pltpu.VMEM((DIFF_HEADS, 1, 2 * tq), F32),
                        pltpu.VMEM((DIFF_HEADS, 1, 2 * tq), F32),
                        pltpu.VMEM((DIFF_HEADS, DIFF_V_DIM + ONES_ROWS, 2 * tq), F32)],
        compiler_params=_cparams(2),
        name="diff_attn",
    )(qt, k, vt, lam_params, sub_gain)


def _gelu_tanh(x):
    return 0.5 * x * (1.0 + jnp.tanh(math.sqrt(2.0 / math.pi) * (x + 0.044715 * (x * x * x))))


def _sigmoid(x):
    return 1.0 / (1.0 + jnp.exp(-x))


def _post_kernel(tok_ref, omem_ref, x_ref, wglu_ref, wout_ref, g_ref, wg_ref, wu_ref,
                 wd_ref, o_ref, *, s5):
    tok = tok_ref[...].T
    if s5:
        h = _gelu_tanh(tok)
        gate = _sigmoid(_dot(h.astype(BF16), wglu_ref[...].astype(BF16)))
        tok = h * gate
    mix = jnp.concatenate([tok.astype(BF16), omem_ref[...].T.astype(BF16)], axis=1)
    x1 = x_ref[...] + _dot(mix, wout_ref[...].astype(BF16))
    h2 = _rms(x1, g_ref[...]).astype(BF16)
    gate = _dot(h2, wg_ref[...])
    up = _dot(h2, wu_ref[...])
    act = (gate * _sigmoid(gate) * up).astype(BF16)
    o_ref[...] = x1 + _dot(act, wd_ref[...])


def _post(tok, o_mem, x, w_glu, w_out, g, wg, wu, wd, *, s5, tm=512):
    n_tok = x.shape[0]
    return pl.pallas_call(
        functools.partial(_post_kernel, s5=s5),
        grid=(n_tok // tm,),
        in_specs=[
            pl.BlockSpec((TOK_WIDTH, tm), lambda i: (0, i)),
            pl.BlockSpec((MEM_WIDTH, tm), lambda i: (0, i)),
            pl.BlockSpec((tm, D_MODEL), lambda i: (i, 0)),
            _const_spec(w_glu.shape),
            _const_spec(w_out.shape),
            _const_spec((1, D_MODEL)),
            _const_spec(wg.shape),
            _const_spec(wu.shape),
            _const_spec(wd.shape),
        ],
        out_specs=pl.BlockSpec((tm, D_MODEL), lambda i: (i, 0)),
        out_shape=jax.ShapeDtypeStruct((n_tok, D_MODEL), F32),
        compiler_params=_cparams(1),
        name="post_s5" if s5 else "post",
    )(tok, o_mem, x, w_glu, w_out, g, wg, wu, wd)


def _tile_gain(g, reps):
    return jnp.tile(g.astype(F32), reps)[None, :]


def kernel(x, mem, norm_mix, norm_ffn, norm_mem, w_in, w_out, mem_wk, mem_wv, mem_q_norm,
           mem_k_norm, ffn_w_gate, ffn_w_up, ffn_w_down, ssm_a_re, ssm_a_im, ssm_log_dt,
           ssm_b_re, ssm_b_im, ssm_c_re, ssm_c_im, ssm_d, ssm_w_glu, kv_norm, diff_wk, diff_wv,
           diff_k_norm, diff_q_norm, diff_lambda_q1, diff_lambda_k1, diff_lambda_q2,
           diff_lambda_k2, diff_sub_norm):
    bsz, seq, _ = x.shape
    n_tok = bsz * seq
    xf = x.reshape(n_tok, D_MODEL)

    k_gain = jnp.tile(mem_k_norm.astype(F32), (1, MEM_HEADS))[:, None, :]
    k_mem, vt_mem = _mem_kv(mem, norm_mem[:, None, :], mem_wk, mem_wv, k_gain)

    k_sh = v_sh = None
    for layer in range(DEPTH):
        s5 = layer < N_A_LAYERS
        q_gain = _tile_gain(mem_q_norm[layer], MEM_HEADS).T
        if s5:
            tok_gain = jnp.ones((TOK_WIDTH, 1), F32)
        else:
            tok_gain = _tile_gain(diff_q_norm[layer - N_A_LAYERS], 2 * DIFF_HEADS).T
        z_tok, o_mem = _in_proj(xf, norm_mix[layer][None, :], w_in[layer],
                                tok_gain, q_gain, k_mem[layer], vt_mem[layer],
                                seq=seq, qk_norm=not s5)
        if s5:
            i = layer
            log_dt = ssm_log_dt[i][:, None, None]
            a_row = jnp.tile(jnp.stack([ssm_a_re[i], ssm_a_im[i]], axis=1), (1, 1, 2))
            a_col = jnp.stack([ssm_a_re[i], ssm_a_im[i]], axis=2)
            b_t = jnp.tile(jnp.stack([ssm_b_re[i], ssm_b_im[i]], axis=1).transpose(0, 1, 3, 2),
                           (1, 1, 1, 2))
            c_t = jnp.stack([ssm_c_re[i], ssm_c_im[i]], axis=1).transpose(0, 1, 3, 2)
            m_op, w_op, v_op, ac = _ssm_prep(log_dt, a_row, a_col, b_t, c_t)
            tok = _ssm_scan(z_tok, m_op, w_op, v_op, ac, ssm_d[i][:, None], bsz=bsz)
            w_glu = ssm_w_glu[i]
        else:
            j = layer - N_A_LAYERS
            lambda_init = 0.8 - 0.6 * math.exp(-0.3 * layer)
            lam_params = jnp.stack([diff_lambda_q1[j], diff_lambda_k1[j],
                                    diff_lambda_q2[j], diff_lambda_k2[j]]).astype(F32)
            tok = _diff_attn(z_tok, k_sh, v_sh, lam_params,
                             _tile_gain(diff_sub_norm[j], DIFF_HEADS).T,
                             bsz=bsz, seq=seq, lambda_init=lambda_init)
            w_glu = jnp.zeros((8, LANES), F32)
        xf = _post(tok, o_mem, xf, w_glu, w_out[layer],
                   norm_ffn[layer][None, :], ffn_w_gate[layer].astype(BF16),
                   ffn_w_up[layer].astype(BF16), ffn_w_down[layer].astype(BF16), s5=s5)
        if layer == N_A_LAYERS - 1:
            k_sh, v_sh = _shared_kv(xf, kv_norm[None, :], diff_wk, diff_wv,
                                    _tile_gain(diff_k_norm, 2 * DIFF_HEADS))
    return xf.reshape(bsz, seq, D_MODEL)
```

```python
import functools
import math

import jax
import jax.numpy as jnp
from jax import lax
from jax.experimental import pallas as pl
from jax.experimental.pallas import tpu as pltpu

D_MODEL = 1024
DEPTH = 4
CHUNK = 64
N_A_LAYERS = DEPTH // 2
MEM_HEADS = 4
MEM_HEAD_DIM = 64
MEM_WIDTH = MEM_HEADS * MEM_HEAD_DIM
TOK_WIDTH = D_MODEL - MEM_WIDTH
SSM_P = 16
SSM_G = TOK_WIDTH // SSM_P
SSM_N = 64
SSM_C = 64
SSM_K = SSM_P * SSM_C
DIFF_HEAD_DIM = 64
DIFF_HEADS = TOK_WIDTH // (2 * DIFF_HEAD_DIM)
DIFF_V_DIM = 2 * DIFF_HEAD_DIM
D_FF = 2816
EPS = 1e-6

LANES = 128
VMEM_LIMIT = 56 * 1024 * 1024

F32 = jnp.float32
BF16 = jnp.bfloat16


def _cparams(n_grid, vmem=VMEM_LIMIT, flags=None):
    return pltpu.CompilerParams(
        dimension_semantics=("arbitrary",) * n_grid, vmem_limit_bytes=vmem, flags=flags)


def _const_spec(shape):
    nd = len(shape)
    return pl.BlockSpec(shape, lambda *_: (0,) * nd, pipeline_mode=pl.Buffered(1))


def _layer_spec(shape, layer):
    nd = len(shape)
    return pl.BlockSpec((1,) + tuple(shape[1:]), lambda *_: (layer,) + (0,) * (nd - 1),
                        pipeline_mode=pl.Buffered(1))


def _rms(x, g):
    return x * lax.rsqrt(jnp.mean(x * x, axis=-1, keepdims=True) + EPS) * g


def _seg_rms64(x, g):
    rows, width = x.shape
    lane = lax.broadcasted_iota(jnp.int32, (rows, LANES), 1)
    lo = lane < 64
    outs = []
    for v in range(width // LANES):
        xv = x[:, v * LANES:(v + 1) * LANES]
        sq = xv * xv
        s_lo = jnp.sum(jnp.where(lo, sq, 0.0), axis=-1, keepdims=True)
        s_hi = jnp.sum(jnp.where(lo, 0.0, sq), axis=-1, keepdims=True)
        r_lo = lax.rsqrt(s_lo * (1.0 / 64) + EPS)
        r_hi = lax.rsqrt(s_hi * (1.0 / 64) + EPS)
        outs.append(xv * jnp.where(lo, r_lo, r_hi) * g[:, v * LANES:(v + 1) * LANES])
    return jnp.concatenate(outs, axis=1)


def _half_masks(rows, dtype):
    lane = lax.broadcasted_iota(jnp.int32, (rows, LANES), 1)
    lo = (lane < 64).astype(dtype)
    return lo, (1 - lo).astype(dtype)


def _dot(a, b):
    return jnp.dot(a, b, preferred_element_type=F32)


def _dot_nt(a, b):
    return lax.dot_general(a, b, (((1,), (1,)), ((), ())), preferred_element_type=F32)


def _mem_kv_kernel(mem_ref, g_ref, wk_ref, wv_ref, kg_ref, k_ref, vt_ref):
    mn = _rms(mem_ref[0], g_ref[0]).astype(BF16)
    k_ref[0, 0] = _seg_rms64(_dot(mn, wk_ref[0].astype(BF16)), kg_ref[0]).astype(BF16)
    vt_ref[0, 0] = _dot(mn, wv_ref[0].astype(BF16)).T.astype(BF16)


def _mem_kv(mem, norm_mem, wk, wv, k_gain):
    bsz, n_mem, _ = mem.shape
    return pl.pallas_call(
        _mem_kv_kernel,
        grid=(DEPTH, bsz),
        in_specs=[
            pl.BlockSpec((1, n_mem, D_MODEL), lambda l, b: (b, 0, 0)),
            pl.BlockSpec((1, 1, D_MODEL), lambda l, b: (l, 0, 0)),
            pl.BlockSpec((1, D_MODEL, MEM_WIDTH), lambda l, b: (l, 0, 0)),
            pl.BlockSpec((1, D_MODEL, MEM_WIDTH), lambda l, b: (l, 0, 0)),
            pl.BlockSpec((1, 1, MEM_WIDTH), lambda l, b: (l, 0, 0)),
        ],
        out_specs=[pl.BlockSpec((1, 1, n_mem, MEM_WIDTH), lambda l, b: (l, b, 0, 0)),
                   pl.BlockSpec((1, 1, MEM_WIDTH, n_mem), lambda l, b: (l, b, 0, 0))],
        out_shape=[jax.ShapeDtypeStruct((DEPTH, bsz, n_mem, MEM_WIDTH), BF16),
                   jax.ShapeDtypeStruct((DEPTH, bsz, MEM_WIDTH, n_mem), BF16)],
        compiler_params=_cparams(2),
        name="mem_kv",
    )(mem, norm_mem, wk, wv, k_gain)


def _seg_rms_rows(xt, g_col, seg):
    width, cols = xt.shape
    x3 = xt.reshape(width // seg, seg, cols)
    ms = jnp.mean(x3 * x3, axis=1, keepdims=True)
    return (x3 * lax.rsqrt(ms + EPS)).reshape(width, cols) * g_col


ONES_ROWS = 16
SOFTMAX_LOG2E = math.log2(math.e)


def _with_ones_rows(vt):
    return jnp.concatenate([vt, jnp.ones((ONES_ROWS, vt.shape[1]), vt.dtype)], axis=0)


def _mem_attention_t(qt, k, vt):
    tm = qt.shape[1]
    first = lax.broadcasted_iota(jnp.int32, (LANES, tm), 0) < MEM_HEAD_DIM
    outs = []
    for grp in range(MEM_WIDTH // LANES):
        rows = slice(grp * LANES, (grp + 1) * LANES)
        qg = qt[rows, :]
        zero = jnp.zeros_like(qg)
        qm = jnp.concatenate([jnp.where(first, qg, zero), jnp.where(first, zero, qg)], axis=1)
        s = _dot(k[:, rows], qm)
        p = jnp.exp2(s - jnp.max(s, axis=0, keepdims=True))
        o2 = _dot(_with_ones_rows(vt[rows, :]), p.astype(BF16))
        o2 = o2[:LANES] * (1.0 / o2[LANES:LANES + 1])
        outs.append(jnp.where(first, o2[:, :tm], o2[:, tm:]))
    return jnp.concatenate(outs, axis=0)


def _in_proj_kernel(x_ref, g_ref, win_ref, tokg_ref, qg_ref, k_ref, vt_ref,
                    ztok_ref, omem_ref, wt_scr, *, qk_norm):
    @pl.when(pl.program_id(0) == 0)
    def _():
        wt_scr[...] = win_ref[0].T.astype(BF16)

    h = _rms(x_ref[...], g_ref[...]).astype(BF16)
    zt = _dot_nt(wt_scr[...], h)
    z_tok = zt[:TOK_WIDTH]
    if qk_norm:
        z_tok = (_seg_rms_rows(z_tok, tokg_ref[...], DIFF_HEAD_DIM)
                 * (DIFF_HEAD_DIM ** -0.5 * SOFTMAX_LOG2E))
    ztok_ref[...] = z_tok.astype(ztok_ref.dtype)
    q_mem = (_seg_rms_rows(zt[TOK_WIDTH:], qg_ref[...], MEM_HEAD_DIM)
             * (MEM_HEAD_DIM ** -0.5 * SOFTMAX_LOG2E))
    omem_ref[...] = _mem_attention_t(q_mem.astype(BF16), k_ref[0, 0],
                                     vt_ref[0, 0]).astype(omem_ref.dtype)


def _in_proj(x, g, w_in, tok_gain, q_gain, k_mem, vt_mem, *, layer, seq, qk_norm, tm=512):
    n_tok = x.shape[0]
    n_mem = k_mem.shape[2]
    tiles_per_seq = seq // tm
    return pl.pallas_call(
        functools.partial(_in_proj_kernel, qk_norm=qk_norm),
        grid=(n_tok // tm,),
        in_specs=[
            pl.BlockSpec((tm, D_MODEL), lambda i: (i, 0)),
            _const_spec((1, D_MODEL)),
            _layer_spec(w_in.shape, layer),
            _const_spec((TOK_WIDTH, 1)),
            _const_spec((MEM_WIDTH, 1)),
            pl.BlockSpec((1, 1, n_mem, MEM_WIDTH), lambda i: (layer, i // tiles_per_seq, 0, 0)),
            pl.BlockSpec((1, 1, MEM_WIDTH, n_mem), lambda i: (layer, i // tiles_per_seq, 0, 0)),
        ],
        out_specs=[pl.BlockSpec((TOK_WIDTH, tm), lambda i: (0, i)),
                   pl.BlockSpec((MEM_WIDTH, tm), lambda i: (0, i))],
        out_shape=[jax.ShapeDtypeStruct((TOK_WIDTH, n_tok), BF16 if qk_norm else F32),
                   jax.ShapeDtypeStruct((MEM_WIDTH, n_tok), F32)],
        scratch_shapes=[pltpu.VMEM((D_MODEL, D_MODEL), BF16)],
        compiler_params=_cparams(1),
        name="in_proj_q" if qk_norm else "in_proj_u",
    )(x, g, w_in, tok_gain, q_gain, k_mem, vt_mem)


def _cpow(ar, ai, dt, k):
    mag = jnp.exp(k * dt * ar)
    ang = k * dt * ai
    return mag * jnp.cos(ang), mag * jnp.sin(ang)


def _zoh_coef(ar, ai, abr, abi):
    den = ar * ar + ai * ai
    nr = abr - 1.0
    return (nr * ar + abi * ai) / den, (abi * ar - nr * ai) / den


def _ssm_prep_kernel(ldt_ref, arow_ref, acol_ref, bt_ref, ct_ref, m_ref, w_ref, v_ref, ac_ref):
    dt = jnp.exp(ldt_ref[0])
    first = lax.broadcasted_iota(jnp.int32, (1, LANES), 1) < SSM_N
    ar_r, ai_r = arow_ref[0, 0:1, :], arow_ref[0, 1:2, :]
    ar_c, ai_c = acol_ref[0, :, 0:1], acol_ref[0, :, 1:2]

    s_row = lax.broadcasted_iota(jnp.int32, (SSM_C, LANES), 0).astype(F32)
    e_re, e_im = _cpow(ar_r, ai_r, dt, float(SSM_C - 1) - s_row)
    ab_re, ab_im = e_re[SSM_C - 2:SSM_C - 1], e_im[SSM_C - 2:SSM_C - 1]
    cf_re, cf_im = _zoh_coef(ar_r, ai_r, ab_re, ab_im)

    pc_re = e_re[0:1] * ab_re - e_im[0:1] * ab_im
    pc_im = e_re[0:1] * ab_im + e_im[0:1] * ab_re
    ac_ref[0, 0:1, :] = pc_re
    ac_ref[0, 1:2, :] = jnp.where(first, -pc_im, pc_im)

    bre, bim = bt_ref[0, 0], bt_ref[0, 1]
    bb_re = cf_re * bre - cf_im * bim
    bb_im = cf_re * bim + cf_im * bre

    for p in range(SSM_P):
        br, bi = bb_re[p:p + 1, :], bb_im[p:p + 1, :]
        w_ref[0, p * SSM_C:(p + 1) * SSM_C, :] = jnp.where(
            first, br * e_re - bi * e_im, br * e_im + bi * e_re).astype(BF16)

    lane_n = lax.broadcasted_iota(jnp.int32, (SSM_N, LANES), 1)
    tau = (lane_n % SSM_C).astype(F32)
    p0_re, p0_im = _cpow(ar_c, ai_c, dt, tau)
    ab_re_c, ab_im_c = p0_re[:, 1:2], p0_im[:, 1:2]
    p1_re = p0_re * ab_re_c - p0_im * ab_im_c
    p1_im = p0_re * ab_im_c + p0_im * ab_re_c
    reps = SSM_K // LANES
    p0_re, p0_im = jnp.concatenate([p0_re] * reps, 1), jnp.concatenate([p0_im] * reps, 1)
    p1_re, p1_im = jnp.concatenate([p1_re] * reps, 1), jnp.concatenate([p1_im] * reps, 1)

    sel = (lax.broadcasted_iota(jnp.int32, (SSM_P, SSM_K), 1) // SSM_C
           == lax.broadcasted_iota(jnp.int32, (SSM_P, SSM_K), 0)).astype(F32)

    def expand(ct):
        return jnp.dot(ct, sel, precision=lax.Precision.HIGHEST, preferred_element_type=F32)

    c_re, c_im = expand(ct_ref[0, 0]), expand(ct_ref[0, 1])

    v_ref[0, 0:SSM_N, :] = (c_re * p1_re - c_im * p1_im).astype(BF16)
    v_ref[0, SSM_N:2 * SSM_N, :] = (-(c_re * p1_im + c_im * p1_re)).astype(BF16)

    ce = jnp.concatenate([c_re * p0_re - c_im * p0_im, -(c_re * p0_im + c_im * p0_re)], axis=0)
    kt = jnp.dot(jnp.where(first, bb_re, bb_im), ce, precision=lax.Precision.HIGHEST,
                 preferred_element_type=F32)

    row = lax.broadcasted_iota(jnp.int32, (SSM_C, SSM_K), 0)
    lane_t = lax.broadcasted_iota(jnp.int32, (SSM_C, SSM_K), 1) % SSM_C
    causal = lane_t >= row
    for p in range(SSM_P):
        base = jnp.broadcast_to(kt[p:p + 1, :], (SSM_C, SSM_K))
        shifted = pltpu.roll(base, 0, 1, stride=1, stride_axis=0)
        m_ref[0, p * SSM_C:(p + 1) * SSM_C, :] = jnp.where(causal, shifted, 0.0).astype(BF16)


def _ssm_prep(log_dt, a_row, a_col, b_t, c_t):
    g = log_dt.shape[0]
    return pl.pallas_call(
        _ssm_prep_kernel,
        grid=(g,),
        in_specs=[
            pl.BlockSpec((1, 1, 1), lambda i: (i, 0, 0)),
            pl.BlockSpec((1, 2, 2 * SSM_N), lambda i: (i, 0, 0)),
            pl.BlockSpec((1, SSM_N, 2), lambda i: (i, 0, 0)),
            pl.BlockSpec((1, 2, SSM_P, 2 * SSM_N), lambda i: (i, 0, 0, 0)),
            pl.BlockSpec((1, 2, SSM_N, SSM_P), lambda i: (i, 0, 0, 0)),
        ],
        out_specs=[
            pl.BlockSpec((1, SSM_K, SSM_K), lambda i: (i, 0, 0)),
            pl.BlockSpec((1, SSM_K, 2 * SSM_N), lambda i: (i, 0, 0)),
            pl.BlockSpec((1, 2 * SSM_N, SSM_K), lambda i: (i, 0, 0)),
            pl.BlockSpec((1, 2, 2 * SSM_N), lambda i: (i, 0, 0)),
        ],
        out_shape=[
            jax.ShapeDtypeStruct((g, SSM_K, SSM_K), BF16),
            jax.ShapeDtypeStruct((g, SSM_K, 2 * SSM_N), BF16),
            jax.ShapeDtypeStruct((g, 2 * SSM_N, SSM_K), BF16),
            jax.ShapeDtypeStruct((g, 2, 2 * SSM_N), F32),
        ],
        compiler_params=_cparams(1),
        name="ssm_prep",
    )(log_dt, a_row, a_col, b_t, c_t)


def _ssm_scan_kernel(u_ref, m_ref, w_ref, v_ref, ac_ref, d_ref, y_ref, z_scr, zs_scr, sp_scr,
                     *, bsz, n_chunks):
    pairs = n_chunks // 2
    n_pairs = bsz * pairs
    ut = u_ref[...]
    u2 = ut.reshape(SSM_P, n_pairs, LANES).transpose(1, 0, 2).reshape(n_pairs, SSM_P * LANES)
    low = lax.broadcasted_iota(jnp.int32, (n_pairs, LANES), 1) < SSM_C

    def col(a, j):
        return a[:, j * LANES:(j + 1) * LANES]

    def swap(a):
        return pltpu.roll(a, SSM_C, 1)

    even = [jnp.where(low, col(u2, 2 * k), swap(col(u2, 2 * k + 1))) for k in range(SSM_P // 2)]
    odd = [jnp.where(low, swap(col(u2, 2 * k)), col(u2, 2 * k + 1)) for k in range(SSM_P // 2)]
    ub = jnp.concatenate([jnp.concatenate(even, axis=1), jnp.concatenate(odd, axis=1)],
                         axis=0).astype(BF16)
    y = _dot(ub, m_ref[0])
    z = _dot(ub, w_ref[0])
    z_scr[...] = z
    zs_scr[...] = pltpu.roll(z, SSM_N, 1)

    a1 = ac_ref[0, 0:1, :]
    a2 = ac_ref[0, 1:2, :]
    s = jnp.zeros((bsz, 2 * SSM_N), F32)
    ss = jnp.zeros((bsz, 2 * SSM_N), F32)
    for c in range(n_chunks):
        rows = pl.ds((c % 2) * n_pairs + c // 2, bsz, stride=pairs)
        sp_scr[rows, :] = s
        if c + 1 < n_chunks:
            s, ss = (a1 * s + a2 * ss + z_scr[rows, :],
                     a1 * ss - a2 * s + zs_scr[rows, :])
    y = y + _dot(sp_scr[...].astype(BF16), v_ref[0])
    y_even, y_odd = y[:n_pairs], y[n_pairs:]
    cols = []
    for k in range(SSM_P // 2):
        ce, co = col(y_even, k), col(y_odd, k)
        cols += [jnp.where(low, ce, swap(co)), jnp.where(low, swap(ce), co)]
    y2 = jnp.concatenate(cols, axis=1)
    yt = y2.reshape(n_pairs, SSM_P, LANES).transpose(1, 0, 2).reshape(SSM_P, n_pairs * LANES)
    y_ref[...] = yt + d_ref[...] * ut


def _ssm_scan(zt, m, w, v, ac, d_col, *, bsz):
    n_tok = zt.shape[1]
    n_rows = n_tok // SSM_C
    n_chunks = n_rows // bsz
    kern = functools.partial(_ssm_scan_kernel, bsz=bsz, n_chunks=n_chunks)
    return pl.pallas_call(
        kern,
        grid=(SSM_G,),
        in_specs=[
            pl.BlockSpec((SSM_P, n_tok), lambda i: (i, 0)),
            pl.BlockSpec((1, SSM_K, SSM_K), lambda i: (i, 0, 0)),
            pl.BlockSpec((1, SSM_K, 2 * SSM_N), lambda i: (i, 0, 0)),
            pl.BlockSpec((1, 2 * SSM_N, SSM_K), lambda i: (i, 0, 0)),
            pl.BlockSpec((1, 2, 2 * SSM_N), lambda i: (i, 0, 0)),
            pl.BlockSpec((SSM_P, 1), lambda i: (i, 0)),
        ],
        out_specs=pl.BlockSpec((SSM_P, n_tok), lambda i: (i, 0)),
        out_shape=jax.ShapeDtypeStruct((TOK_WIDTH, n_tok), F32),
        scratch_shapes=[pltpu.VMEM((n_rows, 2 * SSM_N), F32)] * 3,
        compiler_params=_cparams(1),
        name="ssm_scan",
    )(zt, m, w, v, ac, d_col)


ATTN_BLOCK = 256
KV_SUB = 2


def _shared_kv_kernel(x_ref, g_ref, wk_ref, wv_ref, kg_ref, k_ref, vt_ref):
    h = _rms(x_ref[...], g_ref[...]).astype(BF16)
    k_ref[...] = _seg_rms64(_dot(h, wk_ref[...].astype(BF16)), kg_ref[...]).astype(BF16)
    v = _dot(h, wv_ref[...].astype(BF16))
    for j in range(vt_ref.shape[0]):
        vt_ref[j] = v[j * ATTN_BLOCK:(j + 1) * ATTN_BLOCK].T.astype(BF16)


def _shared_kv(x, g, wk, wv, k_gain, tm=512):
    n_tok = x.shape[0]
    sub = tm // ATTN_BLOCK
    return pl.pallas_call(
        _shared_kv_kernel,
        grid=(n_tok // tm,),
        in_specs=[
            pl.BlockSpec((tm, D_MODEL), lambda i: (i, 0)),
            _const_spec((1, D_MODEL)),
            _const_spec(wk.shape),
            _const_spec(wv.shape),
            _const_spec((1, TOK_WIDTH)),
        ],
        out_specs=[pl.BlockSpec((tm, TOK_WIDTH), lambda i: (i, 0)),
                   pl.BlockSpec((sub, TOK_WIDTH, ATTN_BLOCK), lambda i: (i, 0, 0))],
        out_shape=[jax.ShapeDtypeStruct((n_tok, TOK_WIDTH), BF16),
                   jax.ShapeDtypeStruct((n_tok // ATTN_BLOCK, TOK_WIDTH, ATTN_BLOCK), BF16)],
        compiler_params=_cparams(1),
        name="shared_kv",
    )(x, g, wk, wv, k_gain)


N_STREAMS = 2 * DIFF_HEADS


def _diff_attn_kernel(qt_ref, k_ref, vt_ref, lam_ref, subg_ref, o_ref, qm_scr, sc_scr,
                      bm_scr, m_scr, acc_scr, *, lambda_init):
    tq = ATTN_BLOCK
    qi = pl.program_id(1)
    lp = lam_ref[...]
    lam = (jnp.exp(jnp.sum(lp[0:1] * lp[1:2], axis=-1, keepdims=True))
           - jnp.exp(jnp.sum(lp[2:3] * lp[3:4], axis=-1, keepdims=True)) + lambda_init)
    key_chunk = lax.broadcasted_iota(jnp.int32, (tq, tq), 0) // CHUNK
    qry_chunk = lax.broadcasted_iota(jnp.int32, (tq, tq), 1) // CHUNK
    diag_mask = key_chunk <= qry_chunk
    first_half = lax.broadcasted_iota(jnp.int32, (LANES, tq), 0) < DIFF_HEAD_DIM
    n_grp = DIFF_HEADS // 2

    for kg in range(2 * n_grp):
        qg = qt_ref[kg * LANES:(kg + 1) * LANES, :]
        zero = jnp.zeros_like(qg)
        qm_scr[kg, :, 0:tq] = jnp.where(first_half, qg, zero)
        qm_scr[kg, :, tq:2 * tq] = jnp.where(first_half, zero, qg)

    def block(j, diagonal, n_sub=1):
        nk = n_sub * tq
        rows = pl.ds(pl.multiple_of(j * tq, tq), nk)
        for kg in range(2 * n_grp):
            kb = k_ref[rows, kg * LANES:(kg + 1) * LANES]
            st = _dot(kb, qm_scr[kg])
            if diagonal:
                st = jnp.where(jnp.concatenate([diag_mask, diag_mask], axis=1), st, -jnp.inf)
            sc_scr[kg, 0:nk, :] = st
            bm_scr[kg] = jnp.max(st, axis=0, keepdims=True)
        for kg in range(2 * n_grp):
            g = kg % n_grp
            if diagonal:
                m_new = bm_scr[kg]
            else:
                m_old = m_scr[kg]
                m_new = jnp.maximum(m_old, bm_scr[kg])
                alpha = jnp.exp2(m_old - m_new)
            pb = jnp.exp2(sc_scr[kg, 0:nk, :] - m_new).astype(BF16)
            m_scr[kg] = m_new

            def v_rows(h):
                vt = [vt_ref[j + i, h * DIFF_V_DIM:(h + 1) * DIFF_V_DIM, :] for i in range(n_sub)]
                return _with_ones_rows(vt[0] if n_sub == 1 else jnp.concatenate(vt, axis=1))

            pv = jnp.concatenate(
                [_dot(v_rows(h), pb[:, hh * tq:(hh + 1) * tq])
                 for hh, h in enumerate((2 * g, 2 * g + 1))], axis=1)
            acc_scr[kg] = pv if diagonal else alpha * acc_scr[kg] + pv

    def body(t, carry):
        block(KV_SUB * t, False, KV_SUB)
        return carry

    block(qi, True)
    lax.fori_loop(0, qi // KV_SUB, body, 0)
    for r in range(1, KV_SUB):
        @pl.when(qi % KV_SUB >= r)
        def _():
            block(qi - qi % KV_SUB + r - 1, False)

    for head in range(DIFF_HEADS):
        rows = slice(head * DIFF_V_DIM, (head + 1) * DIFF_V_DIM)
        q_cols = slice((head % 2) * tq, (head % 2 + 1) * tq)
        on = []
        for comp in range(2):
            acc = acc_scr[comp * n_grp + head // 2]
            on.append(acc[0:DIFF_V_DIM, q_cols] * (1.0 / acc[DIFF_V_DIM:DIFF_V_DIM + 1, q_cols]))
        o = on[0] - lam * on[1]
        o = o * lax.rsqrt(jnp.mean(o * o, axis=0, keepdims=True) + EPS)
        o_ref[rows, :] = (o * subg_ref[rows, :] * (1.0 - lambda_init)).astype(o_ref.dtype)


def _diff_attn(qt, k, vt, lam_params, sub_gain, *, bsz, seq, lambda_init):
    tq = ATTN_BLOCK
    n_q = seq // tq
    return pl.pallas_call(
        functools.partial(_diff_attn_kernel, lambda_init=lambda_init),
        grid=(bsz, n_q),
        in_specs=[
            pl.BlockSpec((TOK_WIDTH, tq), lambda b, i: (0, b * n_q + i)),
            pl.BlockSpec((seq, TOK_WIDTH), lambda b, i: (b, 0)),
            pl.BlockSpec((n_q, TOK_WIDTH, tq), lambda b, i: (b, 0, 0)),
            pl.BlockSpec((4, DIFF_HEAD_DIM), lambda b, i: (0, 0)),
            pl.BlockSpec((TOK_WIDTH, 1), lambda b, i: (0, 0)),
        ],
        out_specs=pl.BlockSpec((TOK_WIDTH, tq), lambda b, i: (0, b * n_q + i)),
        out_shape=jax.ShapeDtypeStruct((TOK_WIDTH, bsz * seq), F32),
        scratch_shapes=[pltpu.VMEM((DIFF_HEADS, LANES, 2 * tq), BF16),
                        pltpu.VMEM((DIFF_HEADS, KV_SUB * tq, 2 * tq), F32),
                        pltpu.VMEM((DIFF_HEADS, 1, 2 * tq), F32),
                        pltpu.VMEM((DIFF_HEADS, 1, 2 * tq), F32),
                        pltpu.VMEM((DIFF_HEADS, DIFF_V_DIM + ONES_ROWS, 2 * tq), F32)],
        compiler_params=_cparams(2),
        name="diff_attn",
    )(qt, k, vt, lam_params, sub_gain)


def _gelu_tanh(x):
    return 0.5 * x * (1.0 + jnp.tanh(math.sqrt(2.0 / math.pi) * (x + 0.044715 * (x * x * x))))


def _sigmoid(x):
    return 1.0 / (1.0 + jnp.exp(-x))


def _post_kernel(tok_ref, omem_ref, x_ref, wglu_ref, wout_ref, g_ref, wg_ref, wu_ref,
                 wd_ref, o_ref, *, s5):
    tok = tok_ref[...].T
    if s5:
        h = _gelu_tanh(tok)
        gate = _sigmoid(_dot(h.astype(BF16), wglu_ref[0].astype(BF16)))
        tok = h * gate
    mix = jnp.concatenate([tok.astype(BF16), omem_ref[...].T.astype(BF16)], axis=1)
    x1 = x_ref[...] + _dot(mix, wout_ref[0].astype(BF16))
    h2 = _rms(x1, g_ref[...]).astype(BF16)
    gate = _dot(h2, wg_ref[0])
    up = _dot(h2, wu_ref[0])
    act = (gate * _sigmoid(gate) * up).astype(BF16)
    o_ref[...] = x1 + _dot(act, wd_ref[0])


def _post(tok, o_mem, x, w_glu, w_out, g, wg, wu, wd, *, layer, glu_layer, s5, tm=512):
    n_tok = x.shape[0]
    return pl.pallas_call(
        functools.partial(_post_kernel, s5=s5),
        grid=(n_tok // tm,),
        in_specs=[
            pl.BlockSpec((TOK_WIDTH, tm), lambda i: (0, i)),
            pl.BlockSpec((MEM_WIDTH, tm), lambda i: (0, i)),
            pl.BlockSpec((tm, D_MODEL), lambda i: (i, 0)),
            _layer_spec(w_glu.shape, glu_layer),
            _layer_spec(w_out.shape, layer),
            _const_spec((1, D_MODEL)),
            _layer_spec(wg.shape, layer),
            _layer_spec(wu.shape, layer),
            _layer_spec(wd.shape, layer),
        ],
        out_specs=pl.BlockSpec((tm, D_MODEL), lambda i: (i, 0)),
        out_shape=jax.ShapeDtypeStruct((n_tok, D_MODEL), F32),
        compiler_params=_cparams(1),
        name="post_s5" if s5 else "post",
    )(tok, o_mem, x, w_glu, w_out, g, wg, wu, wd)


def _tile_gain(g, reps):
    return jnp.tile(g.astype(F32), reps)[None, :]


def kernel(x, mem, norm_mix, norm_ffn, norm_mem, w_in, w_out, mem_wk, mem_wv, mem_q_norm,
           mem_k_norm, ffn_w_gate, ffn_w_up, ffn_w_down, ssm_a_re, ssm_a_im, ssm_log_dt,
           ssm_b_re, ssm_b_im, ssm_c_re, ssm_c_im, ssm_d, ssm_w_glu, kv_norm, diff_wk, diff_wv,
           diff_k_norm, diff_q_norm, diff_lambda_q1, diff_lambda_k1, diff_lambda_q2,
           diff_lambda_k2, diff_sub_norm):
    bsz, seq, _ = x.shape
    n_tok = bsz * seq
    xf = x.reshape(n_tok, D_MODEL)

    k_gain = jnp.tile(mem_k_norm.astype(F32), (1, MEM_HEADS))[:, None, :]
    k_mem, vt_mem = _mem_kv(mem, norm_mem[:, None, :], mem_wk, mem_wv, k_gain)

    ffn_gate, ffn_up, ffn_down = (w.astype(BF16) for w in (ffn_w_gate, ffn_w_up, ffn_w_down))

    k_sh = v_sh = None
    for layer in range(DEPTH):
        s5 = layer < N_A_LAYERS
        q_gain = _tile_gain(mem_q_norm[layer], MEM_HEADS).T
        if s5:
            tok_gain = jnp.ones((TOK_WIDTH, 1), F32)
        else:
            tok_gain = _tile_gain(diff_q_norm[layer - N_A_LAYERS], 2 * DIFF_HEADS).T
        z_tok, o_mem = _in_proj(xf, norm_mix[layer][None, :], w_in,
                                tok_gain, q_gain, k_mem, vt_mem,
                                layer=layer, seq=seq, qk_norm=not s5)
        if s5:
            i = layer
            log_dt = ssm_log_dt[i][:, None, None]
            a_row = jnp.tile(jnp.stack([ssm_a_re[i], ssm_a_im[i]], axis=1), (1, 1, 2))
            a_col = jnp.stack([ssm_a_re[i], ssm_a_im[i]], axis=2)
            b_t = jnp.tile(jnp.stack([ssm_b_re[i], ssm_b_im[i]], axis=1).transpose(0, 1, 3, 2),
                           (1, 1, 1, 2))
            c_t = jnp.stack([ssm_c_re[i], ssm_c_im[i]], axis=1).transpose(0, 1, 3, 2)
            m_op, w_op, v_op, ac = _ssm_prep(log_dt, a_row, a_col, b_t, c_t)
            tok = _ssm_scan(z_tok, m_op, w_op, v_op, ac, ssm_d[i][:, None], bsz=bsz)
        else:
            j = layer - N_A_LAYERS
            lambda_init = 0.8 - 0.6 * math.exp(-0.3 * layer)
            lam_params = jnp.stack([diff_lambda_q1[j], diff_lambda_k1[j],
                                    diff_lambda_q2[j], diff_lambda_k2[j]]).astype(F32)
            tok = _diff_attn(z_tok, k_sh, v_sh, lam_params,
                             _tile_gain(diff_sub_norm[j], DIFF_HEADS).T,
                             bsz=bsz, seq=seq, lambda_init=lambda_init)
        xf = _post(tok, o_mem, xf, ssm_w_glu, w_out, norm_ffn[layer][None, :],
                   ffn_gate, ffn_up, ffn_down,
                   layer=layer, glu_layer=layer if s5 else 0, s5=s5)
        if layer == N_A_LAYERS - 1:
            k_sh, v_sh = _shared_kv(xf, kv_norm[None, :], diff_wk, diff_wv,
                                    _tile_gain(diff_k_norm, 2 * DIFF_HEADS))
    return xf.reshape(bsz, seq, D_MODEL)
```

```python
import functools
import math

import jax
import jax.numpy as jnp
from jax import lax
from jax.experimental import pallas as pl
from jax.experimental.pallas import tpu as pltpu

D_MODEL = 1024
DEPTH = 4
CHUNK = 64
N_A_LAYERS = DEPTH // 2
MEM_HEADS = 4
MEM_HEAD_DIM = 64
MEM_WIDTH = MEM_HEADS * MEM_HEAD_DIM
TOK_WIDTH = D_MODEL - MEM_WIDTH
SSM_P = 16
SSM_G = TOK_WIDTH // SSM_P
SSM_N = 64
SSM_C = 64
SSM_K = SSM_P * SSM_C
DIFF_HEAD_DIM = 64
DIFF_HEADS = TOK_WIDTH // (2 * DIFF_HEAD_DIM)
DIFF_V_DIM = 2 * DIFF_HEAD_DIM
D_FF = 2816
EPS = 1e-6

LANES = 128
VMEM_LIMIT = 56 * 1024 * 1024

F32 = jnp.float32
BF16 = jnp.bfloat16


def _cparams(n_grid, vmem=VMEM_LIMIT, flags=None):
    return pltpu.CompilerParams(
        dimension_semantics=("arbitrary",) * n_grid, vmem_limit_bytes=vmem, flags=flags)


def _const_spec(shape):
    nd = len(shape)
    return pl.BlockSpec(shape, lambda *_: (0,) * nd, pipeline_mode=pl.Buffered(1))


def _layer_spec(shape, layer):
    nd = len(shape)
    return pl.BlockSpec((1,) + tuple(shape[1:]), lambda *_: (layer,) + (0,) * (nd - 1),
                        pipeline_mode=pl.Buffered(1))


def _rms(x, g):
    return x * lax.rsqrt(jnp.mean(x * x, axis=-1, keepdims=True) + EPS) * g


def _seg_rms64(x, g):
    rows, width = x.shape
    lane = lax.broadcasted_iota(jnp.int32, (rows, LANES), 1)
    lo = lane < 64
    outs = []
    for v in range(width // LANES):
        xv = x[:, v * LANES:(v + 1) * LANES]
        sq = xv * xv
        s_lo = jnp.sum(jnp.where(lo, sq, 0.0), axis=-1, keepdims=True)
        s_hi = jnp.sum(jnp.where(lo, 0.0, sq), axis=-1, keepdims=True)
        r_lo = lax.rsqrt(s_lo * (1.0 / 64) + EPS)
        r_hi = lax.rsqrt(s_hi * (1.0 / 64) + EPS)
        outs.append(xv * jnp.where(lo, r_lo, r_hi) * g[:, v * LANES:(v + 1) * LANES])
    return jnp.concatenate(outs, axis=1)


def _half_masks(rows, dtype):
    lane = lax.broadcasted_iota(jnp.int32, (rows, LANES), 1)
    lo = (lane < 64).astype(dtype)
    return lo, (1 - lo).astype(dtype)


def _dot(a, b):
    return jnp.dot(a, b, preferred_element_type=F32)


def _dot_nt(a, b):
    return lax.dot_general(a, b, (((1,), (1,)), ((), ())), preferred_element_type=F32)


def _mem_kv_kernel(mem_ref, g_ref, wk_ref, wv_ref, kg_ref, k_ref, vt_ref):
    mn = _rms(mem_ref[0], g_ref[0]).astype(BF16)
    k_ref[0, 0] = _seg_rms64(_dot(mn, wk_ref[0].astype(BF16)), kg_ref[0]).astype(BF16)
    vt_ref[0, 0] = _dot(mn, wv_ref[0].astype(BF16)).T.astype(BF16)


def _mem_kv(mem, norm_mem, wk, wv, k_gain):
    bsz, n_mem, _ = mem.shape
    return pl.pallas_call(
        _mem_kv_kernel,
        grid=(DEPTH, bsz),
        in_specs=[
            pl.BlockSpec((1, n_mem, D_MODEL), lambda l, b: (b, 0, 0)),
            pl.BlockSpec((1, 1, D_MODEL), lambda l, b: (l, 0, 0)),
            pl.BlockSpec((1, D_MODEL, MEM_WIDTH), lambda l, b: (l, 0, 0)),
            pl.BlockSpec((1, D_MODEL, MEM_WIDTH), lambda l, b: (l, 0, 0)),
            pl.BlockSpec((1, 1, MEM_WIDTH), lambda l, b: (l, 0, 0)),
        ],
        out_specs=[pl.BlockSpec((1, 1, n_mem, MEM_WIDTH), lambda l, b: (l, b, 0, 0)),
                   pl.BlockSpec((1, 1, MEM_WIDTH, n_mem), lambda l, b: (l, b, 0, 0))],
        out_shape=[jax.ShapeDtypeStruct((DEPTH, bsz, n_mem, MEM_WIDTH), BF16),
                   jax.ShapeDtypeStruct((DEPTH, bsz, MEM_WIDTH, n_mem), BF16)],
        compiler_params=_cparams(2),
        name="mem_kv",
    )(mem, norm_mem, wk, wv, k_gain)


def _seg_rms_rows(xt, g_col, seg):
    width, cols = xt.shape
    x3 = xt.reshape(width // seg, seg, cols)
    ms = jnp.mean(x3 * x3, axis=1, keepdims=True)
    return (x3 * lax.rsqrt(ms + EPS)).reshape(width, cols) * g_col


ONES_ROWS = 16
SOFTMAX_LOG2E = math.log2(math.e)


def _with_ones_rows(vt):
    return jnp.concatenate([vt, jnp.ones((ONES_ROWS, vt.shape[1]), vt.dtype)], axis=0)


def _mem_attention_t(qt, k, vt):
    tm = qt.shape[1]
    first = lax.broadcasted_iota(jnp.int32, (LANES, tm), 0) < MEM_HEAD_DIM
    outs = []
    for grp in range(MEM_WIDTH // LANES):
        rows = slice(grp * LANES, (grp + 1) * LANES)
        qg = qt[rows, :]
        zero = jnp.zeros_like(qg)
        qm = jnp.concatenate([jnp.where(first, qg, zero), jnp.where(first, zero, qg)], axis=1)
        s = _dot(k[:, rows], qm)
        p = jnp.exp2(s - jnp.max(s, axis=0, keepdims=True))
        o2 = _dot(_with_ones_rows(vt[rows, :]), p.astype(BF16))
        o2 = o2[:LANES] * (1.0 / o2[LANES:LANES + 1])
        outs.append(jnp.where(first, o2[:, :tm], o2[:, tm:]))
    return jnp.concatenate(outs, axis=0)


def _in_proj_kernel(x_ref, g_ref, win_ref, tokg_ref, qg_ref, k_ref, vt_ref,
                    ztok_ref, omem_ref, wt_scr, *, qk_norm):
    @pl.when(pl.program_id(0) == 0)
    def _():
        wt_scr[...] = win_ref[0].T.astype(BF16)

    h = _rms(x_ref[...], g_ref[...]).astype(BF16)
    zt = _dot_nt(wt_scr[...], h)
    z_tok = zt[:TOK_WIDTH]
    if qk_norm:
        z_tok = (_seg_rms_rows(z_tok, tokg_ref[...], DIFF_HEAD_DIM)
                 * (DIFF_HEAD_DIM ** -0.5 * SOFTMAX_LOG2E))
    ztok_ref[...] = z_tok.astype(ztok_ref.dtype)
    q_mem = (_seg_rms_rows(zt[TOK_WIDTH:], qg_ref[...], MEM_HEAD_DIM)
             * (MEM_HEAD_DIM ** -0.5 * SOFTMAX_LOG2E))
    omem_ref[...] = _mem_attention_t(q_mem.astype(BF16), k_ref[0, 0],
                                     vt_ref[0, 0]).astype(omem_ref.dtype)


def _in_proj(x, g, w_in, tok_gain, q_gain, k_mem, vt_mem, *, layer, seq, qk_norm, tm=512):
    n_tok = x.shape[0]
    n_mem = k_mem.shape[2]
    tiles_per_seq = seq // tm
    return pl.pallas_call(
        functools.partial(_in_proj_kernel, qk_norm=qk_norm),
        grid=(n_tok // tm,),
        in_specs=[
            pl.BlockSpec((tm, D_MODEL), lambda i: (i, 0)),
            _const_spec((1, D_MODEL)),
            _layer_spec(w_in.shape, layer),
            _const_spec((TOK_WIDTH, 1)),
            _const_spec((MEM_WIDTH, 1)),
            pl.BlockSpec((1, 1, n_mem, MEM_WIDTH), lambda i: (layer, i // tiles_per_seq, 0, 0)),
            pl.BlockSpec((1, 1, MEM_WIDTH, n_mem), lambda i: (layer, i // tiles_per_seq, 0, 0)),
        ],
        out_specs=[pl.BlockSpec((TOK_WIDTH, tm), lambda i: (0, i)),
                   pl.BlockSpec((MEM_WIDTH, tm), lambda i: (0, i))],
        out_shape=[jax.ShapeDtypeStruct((TOK_WIDTH, n_tok), BF16 if qk_norm else F32),
                   jax.ShapeDtypeStruct((MEM_WIDTH, n_tok), F32)],
        scratch_shapes=[pltpu.VMEM((D_MODEL, D_MODEL), BF16)],
        compiler_params=_cparams(1),
        name="in_proj_q" if qk_norm else "in_proj_u",
    )(x, g, w_in, tok_gain, q_gain, k_mem, vt_mem)


def _cpow(ar, ai, dt, k):
    mag = jnp.exp(k * dt * ar)
    ang = k * dt * ai
    return mag * jnp.cos(ang), mag * jnp.sin(ang)


def _zoh_coef(ar, ai, abr, abi):
    den = ar * ar + ai * ai
    nr = abr - 1.0
    return (nr * ar + abi * ai) / den, (abi * ar - nr * ai) / den


def _ssm_prep_kernel(ldt_ref, arow_ref, acol_ref, bt_ref, ct_ref, m_ref, w_ref, v_ref, ac_ref):
    dt = jnp.exp(ldt_ref[0])
    first = lax.broadcasted_iota(jnp.int32, (1, LANES), 1) < SSM_N
    ar_r, ai_r = arow_ref[0, 0:1, :], arow_ref[0, 1:2, :]
    ar_c, ai_c = acol_ref[0, :, 0:1], acol_ref[0, :, 1:2]

    s_row = lax.broadcasted_iota(jnp.int32, (SSM_C, LANES), 0).astype(F32)
    e_re, e_im = _cpow(ar_r, ai_r, dt, float(SSM_C - 1) - s_row)
    ab_re, ab_im = e_re[SSM_C - 2:SSM_C - 1], e_im[SSM_C - 2:SSM_C - 1]
    cf_re, cf_im = _zoh_coef(ar_r, ai_r, ab_re, ab_im)

    pc_re = e_re[0:1] * ab_re - e_im[0:1] * ab_im
    pc_im = e_re[0:1] * ab_im + e_im[0:1] * ab_re
    ac_ref[0, 0:1, :] = pc_re
    ac_ref[0, 1:2, :] = jnp.where(first, -pc_im, pc_im)

    bre, bim = bt_ref[0, 0], bt_ref[0, 1]
    bb_re = cf_re * bre - cf_im * bim
    bb_im = cf_re * bim + cf_im * bre

    for p in range(SSM_P):
        br, bi = bb_re[p:p + 1, :], bb_im[p:p + 1, :]
        w_ref[0, p * SSM_C:(p + 1) * SSM_C, :] = jnp.where(
            first, br * e_re - bi * e_im, br * e_im + bi * e_re).astype(BF16)

    lane_n = lax.broadcasted_iota(jnp.int32, (SSM_N, LANES), 1)
    tau = (lane_n % SSM_C).astype(F32)
    p0_re, p0_im = _cpow(ar_c, ai_c, dt, tau)
    ab_re_c, ab_im_c = p0_re[:, 1:2], p0_im[:, 1:2]
    p1_re = p0_re * ab_re_c - p0_im * ab_im_c
    p1_im = p0_re * ab_im_c + p0_im * ab_re_c
    reps = SSM_K // LANES
    p0_re, p0_im = jnp.concatenate([p0_re] * reps, 1), jnp.concatenate([p0_im] * reps, 1)
    p1_re, p1_im = jnp.concatenate([p1_re] * reps, 1), jnp.concatenate([p1_im] * reps, 1)

    sel = (lax.broadcasted_iota(jnp.int32, (SSM_P, SSM_K), 1) // SSM_C
           == lax.broadcasted_iota(jnp.int32, (SSM_P, SSM_K), 0)).astype(F32)

    def expand(ct):
        return jnp.dot(ct, sel, precision=lax.Precision.HIGHEST, preferred_element_type=F32)

    c_re, c_im = expand(ct_ref[0, 0]), expand(ct_ref[0, 1])

    v_ref[0, 0:SSM_N, :] = (c_re * p1_re - c_im * p1_im).astype(BF16)
    v_ref[0, SSM_N:2 * SSM_N, :] = (-(c_re * p1_im + c_im * p1_re)).astype(BF16)

    ce = jnp.concatenate([c_re * p0_re - c_im * p0_im, -(c_re * p0_im + c_im * p0_re)], axis=0)
    kt = jnp.dot(jnp.where(first, bb_re, bb_im), ce, precision=lax.Precision.HIGHEST,
                 preferred_element_type=F32)

    row = lax.broadcasted_iota(jnp.int32, (SSM_C, SSM_K), 0)
    lane_t = lax.broadcasted_iota(jnp.int32, (SSM_C, SSM_K), 1) % SSM_C
    causal = lane_t >= row
    for p in range(SSM_P):
        base = jnp.broadcast_to(kt[p:p + 1, :], (SSM_C, SSM_K))
        shifted = pltpu.roll(base, 0, 1, stride=1, stride_axis=0)
        m_ref[0, p * SSM_C:(p + 1) * SSM_C, :] = jnp.where(causal, shifted, 0.0).astype(BF16)


def _ssm_prep(log_dt, a_row, a_col, b_t, c_t):
    g = log_dt.shape[0]
    return pl.pallas_call(
        _ssm_prep_kernel,
        grid=(g,),
        in_specs=[
            pl.BlockSpec((1, 1, 1), lambda i: (i, 0, 0)),
            pl.BlockSpec((1, 2, 2 * SSM_N), lambda i: (i, 0, 0)),
            pl.BlockSpec((1, SSM_N, 2), lambda i: (i, 0, 0)),
            pl.BlockSpec((1, 2, SSM_P, 2 * SSM_N), lambda i: (i, 0, 0, 0)),
            pl.BlockSpec((1, 2, SSM_N, SSM_P), lambda i: (i, 0, 0, 0)),
        ],
        out_specs=[
            pl.BlockSpec((1, SSM_K, SSM_K), lambda i: (i, 0, 0)),
            pl.BlockSpec((1, SSM_K, 2 * SSM_N), lambda i: (i, 0, 0)),
            pl.BlockSpec((1, 2 * SSM_N, SSM_K), lambda i: (i, 0, 0)),
            pl.BlockSpec((1, 2, 2 * SSM_N), lambda i: (i, 0, 0)),
        ],
        out_shape=[
            jax.ShapeDtypeStruct((g, SSM_K, SSM_K), BF16),
            jax.ShapeDtypeStruct((g, SSM_K, 2 * SSM_N), BF16),
            jax.ShapeDtypeStruct((g, 2 * SSM_N, SSM_K), BF16),
            jax.ShapeDtypeStruct((g, 2, 2 * SSM_N), F32),
        ],
        compiler_params=_cparams(1),
        name="ssm_prep",
    )(log_dt, a_row, a_col, b_t, c_t)


SUBLANES = 8


def _transpose8(tiles):
    sub = lax.broadcasted_iota(jnp.int32, tiles[0].shape, 0)
    t = list(tiles)
    for d in (4, 2, 1):
        clear = (sub & d) == 0
        for i0 in range(SUBLANES):
            if i0 & d:
                continue
            lo, hi = t[i0], t[i0 + d]
            t[i0] = jnp.where(clear, lo, pltpu.roll(hi, d, 0))
            t[i0 + d] = jnp.where(clear, pltpu.roll(lo, SUBLANES - d, 0), hi)
    return t


def _rows_to_lane_blocks(x):
    n_rows, width = x.shape
    n = width // LANES
    out_rows = []
    for k in range(n // SUBLANES):
        pieces = []
        for b in range(n_rows // SUBLANES):
            xb = x[b * SUBLANES:(b + 1) * SUBLANES]
            pieces += _transpose8([xb[:, (SUBLANES * k + i) * LANES:(SUBLANES * k + i + 1) * LANES]
                                   for i in range(SUBLANES)])
        out_rows.append(jnp.concatenate(pieces, axis=1))
    return jnp.concatenate(out_rows, axis=0)


def _lane_blocks_to_rows(y, n_rows):
    n = y.shape[0]
    out_rows = []
    for b in range(n_rows // SUBLANES):
        pieces = []
        for k in range(n // SUBLANES):
            yk = y[k * SUBLANES:(k + 1) * SUBLANES]
            pieces += _transpose8([yk[:, (SUBLANES * b + j) * LANES:(SUBLANES * b + j + 1) * LANES]
                                   for j in range(SUBLANES)])
        out_rows.append(jnp.concatenate(pieces, axis=1))
    return jnp.concatenate(out_rows, axis=0)

def _ssm_scan_kernel(u_ref, m_ref, w_ref, v_ref, ac_ref, d_ref, y_ref, z_scr, zs_scr, sp_scr,
                     *, bsz, n_chunks):
    pairs = n_chunks // 2
    n_pairs = bsz * pairs
    ut = u_ref[...]
    u2 = _rows_to_lane_blocks(ut)
    low = lax.broadcasted_iota(jnp.int32, (n_pairs, LANES), 1) < SSM_C

    def col(a, j):
        return a[:, j * LANES:(j + 1) * LANES]

    def swap(a):
        return pltpu.roll(a, SSM_C, 1)

    even = [jnp.where(low, col(u2, 2 * k), swap(col(u2, 2 * k + 1))) for k in range(SSM_P // 2)]
    odd = [jnp.where(low, swap(col(u2, 2 * k)), col(u2, 2 * k + 1)) for k in range(SSM_P // 2)]
    ub = jnp.concatenate([jnp.concatenate(even, axis=1), jnp.concatenate(odd, axis=1)],
                         axis=0).astype(BF16)
    y = _dot(ub, m_ref[0])
    z = _dot(ub, w_ref[0])
    z_scr[...] = z
    zs_scr[...] = pltpu.roll(z, SSM_N, 1)

    a1 = ac_ref[0, 0:1, :]
    a2 = ac_ref[0, 1:2, :]
    s = jnp.zeros((bsz, 2 * SSM_N), F32)
    ss = jnp.zeros((bsz, 2 * SSM_N), F32)
    for c in range(n_chunks):
        rows = pl.ds((c % 2) * n_pairs + c // 2, bsz, stride=pairs)
        sp_scr[rows, :] = s
        if c + 1 < n_chunks:
            s, ss = (a1 * s + a2 * ss + z_scr[rows, :],
                     a1 * ss - a2 * s + zs_scr[rows, :])
    y = y + _dot(sp_scr[...].astype(BF16), v_ref[0])
    y_even, y_odd = y[:n_pairs], y[n_pairs:]
    cols = []
    for k in range(SSM_P // 2):
        ce, co = col(y_even, k), col(y_odd, k)
        cols += [jnp.where(low, ce, swap(co)), jnp.where(low, swap(ce), co)]
    y2 = jnp.concatenate(cols, axis=1)
    yt = _lane_blocks_to_rows(y2, SSM_P)
    y_ref[...] = yt + d_ref[...] * ut


def _ssm_scan(zt, m, w, v, ac, d_col, *, bsz):
    n_tok = zt.shape[1]
    n_rows = n_tok // SSM_C
    n_chunks = n_rows // bsz
    kern = functools.partial(_ssm_scan_kernel, bsz=bsz, n_chunks=n_chunks)
    return pl.pallas_call(
        kern,
        grid=(SSM_G,),
        in_specs=[
            pl.BlockSpec((SSM_P, n_tok), lambda i: (i, 0)),
            pl.BlockSpec((1, SSM_K, SSM_K), lambda i: (i, 0, 0)),
            pl.BlockSpec((1, SSM_K, 2 * SSM_N), lambda i: (i, 0, 0)),
            pl.BlockSpec((1, 2 * SSM_N, SSM_K), lambda i: (i, 0, 0)),
            pl.BlockSpec((1, 2, 2 * SSM_N), lambda i: (i, 0, 0)),
            pl.BlockSpec((SSM_P, 1), lambda i: (i, 0)),
        ],
        out_specs=pl.BlockSpec((SSM_P, n_tok), lambda i: (i, 0)),
        out_shape=jax.ShapeDtypeStruct((TOK_WIDTH, n_tok), F32),
        scratch_shapes=[pltpu.VMEM((n_rows, 2 * SSM_N), F32)] * 3,
        compiler_params=_cparams(1),
        name="ssm_scan",
    )(zt, m, w, v, ac, d_col)


ATTN_BLOCK = 256
KV_SUB = 2


def _shared_kv_kernel(x_ref, g_ref, wk_ref, wv_ref, kg_ref, k_ref, vt_ref):
    h = _rms(x_ref[...], g_ref[...]).astype(BF16)
    k_ref[...] = _seg_rms64(_dot(h, wk_ref[...].astype(BF16)), kg_ref[...]).astype(BF16)
    v = _dot(h, wv_ref[...].astype(BF16))
    for j in range(vt_ref.shape[0]):
        vt_ref[j] = v[j * ATTN_BLOCK:(j + 1) * ATTN_BLOCK].T.astype(BF16)


def _shared_kv(x, g, wk, wv, k_gain, tm=512):
    n_tok = x.shape[0]
    sub = tm // ATTN_BLOCK
    return pl.pallas_call(
        _shared_kv_kernel,
        grid=(n_tok // tm,),
        in_specs=[
            pl.BlockSpec((tm, D_MODEL), lambda i: (i, 0)),
            _const_spec((1, D_MODEL)),
            _const_spec(wk.shape),
            _const_spec(wv.shape),
            _const_spec((1, TOK_WIDTH)),
        ],
        out_specs=[pl.BlockSpec((tm, TOK_WIDTH), lambda i: (i, 0)),
                   pl.BlockSpec((sub, TOK_WIDTH, ATTN_BLOCK), lambda i: (i, 0, 0))],
        out_shape=[jax.ShapeDtypeStruct((n_tok, TOK_WIDTH), BF16),
                   jax.ShapeDtypeStruct((n_tok // ATTN_BLOCK, TOK_WIDTH, ATTN_BLOCK), BF16)],
        compiler_params=_cparams(1),
        name="shared_kv",
    )(x, g, wk, wv, k_gain)


N_STREAMS = 2 * DIFF_HEADS


def _diff_attn_kernel(qt_ref, k_ref, vt_ref, lam_ref, subg_ref, o_ref, qm_scr, sc_scr,
                      bm_scr, m_scr, acc_scr, *, lambda_init):
    tq = ATTN_BLOCK
    qi = pl.program_id(1)
    lp = lam_ref[...]
    lam = (jnp.exp(jnp.sum(lp[0:1] * lp[1:2], axis=-1, keepdims=True))
           - jnp.exp(jnp.sum(lp[2:3] * lp[3:4], axis=-1, keepdims=True)) + lambda_init)
    key_chunk = lax.broadcasted_iota(jnp.int32, (tq, tq), 0) // CHUNK
    qry_chunk = lax.broadcasted_iota(jnp.int32, (tq, tq), 1) // CHUNK
    diag_mask = key_chunk <= qry_chunk
    first_half = lax.broadcasted_iota(jnp.int32, (LANES, tq), 0) < DIFF_HEAD_DIM
    n_grp = DIFF_HEADS // 2

    for kg in range(2 * n_grp):
        qg = qt_ref[kg * LANES:(kg + 1) * LANES, :]
        zero = jnp.zeros_like(qg)
        qm_scr[kg, :, 0:tq] = jnp.where(first_half, qg, zero)
        qm_scr[kg, :, tq:2 * tq] = jnp.where(first_half, zero, qg)

    def block(j, diagonal, n_sub=1):
        nk = n_sub * tq
        rows = pl.ds(pl.multiple_of(j * tq, tq), nk)
        for kg in range(2 * n_grp):
            kb = k_ref[rows, kg * LANES:(kg + 1) * LANES]
            st = _dot(kb, qm_scr[kg])
            if diagonal:
                st = jnp.where(jnp.concatenate([diag_mask, diag_mask], axis=1), st, -jnp.inf)
            sc_scr[kg, 0:nk, :] = st
            bm_scr[kg] = jnp.max(st, axis=0, keepdims=True)
        for kg in range(2 * n_grp):
            g = kg % n_grp
            if diagonal:
                m_new = bm_scr[kg]
            else:
                m_old = m_scr[kg]
                m_new = jnp.maximum(m_old, bm_scr[kg])
                alpha = jnp.exp2(m_old - m_new)
            pb = jnp.exp2(sc_scr[kg, 0:nk, :] - m_new).astype(BF16)
            m_scr[kg] = m_new

            def v_rows(h):
                vt = [vt_ref[j + i, h * DIFF_V_DIM:(h + 1) * DIFF_V_DIM, :] for i in range(n_sub)]
                return _with_ones_rows(vt[0] if n_sub == 1 else jnp.concatenate(vt, axis=1))

            pv = jnp.concatenate(
                [_dot(v_rows(h), pb[:, hh * tq:(hh + 1) * tq])
                 for hh, h in enumerate((2 * g, 2 * g + 1))], axis=1)
            acc_scr[kg] = pv if diagonal else alpha * acc_scr[kg] + pv

    def body(t, carry):
        block(KV_SUB * t, False, KV_SUB)
        return carry

    block(qi, True)
    lax.fori_loop(0, qi // KV_SUB, body, 0)
    for r in range(1, KV_SUB):
        @pl.when(qi % KV_SUB >= r)
        def _():
            block(qi - qi % KV_SUB + r - 1, False)

    for head in range(DIFF_HEADS):
        rows = slice(head * DIFF_V_DIM, (head + 1) * DIFF_V_DIM)
        q_cols = slice((head % 2) * tq, (head % 2 + 1) * tq)
        on = []
        for comp in range(2):
            acc = acc_scr[comp * n_grp + head // 2]
            on.append(acc[0:DIFF_V_DIM, q_cols] * (1.0 / acc[DIFF_V_DIM:DIFF_V_DIM + 1, q_cols]))
        o = on[0] - lam * on[1]
        o = o * lax.rsqrt(jnp.mean(o * o, axis=0, keepdims=True) + EPS)
        o_ref[rows, :] = (o * subg_ref[rows, :] * (1.0 - lambda_init)).astype(o_ref.dtype)


def _diff_attn(qt, k, vt, lam_params, sub_gain, *, bsz, seq, lambda_init):
    tq = ATTN_BLOCK
    n_q = seq // tq
    return pl.pallas_call(
        functools.partial(_diff_attn_kernel, lambda_init=lambda_init),
        grid=(bsz, n_q),
        in_specs=[
            pl.BlockSpec((TOK_WIDTH, tq), lambda b, i: (0, b * n_q + i)),
            pl.BlockSpec((seq, TOK_WIDTH), lambda b, i: (b, 0)),
            pl.BlockSpec((n_q, TOK_WIDTH, tq), lambda b, i: (b, 0, 0)),
            pl.BlockSpec((4, DIFF_HEAD_DIM), lambda b, i: (0, 0)),
            pl.BlockSpec((TOK_WIDTH, 1), lambda b, i: (0, 0)),
        ],
        out_specs=pl.BlockSpec((TOK_WIDTH, tq), lambda b, i: (0, b * n_q + i)),
        out_shape=jax.ShapeDtypeStruct((TOK_WIDTH, bsz * seq), F32),
        scratch_shapes=[pltpu.VMEM((DIFF_HEADS, LANES, 2 * tq), BF16),
                        pltpu.VMEM((DIFF_HEADS, KV_SUB * tq, 2 * tq), F32),
                        pltpu.VMEM((DIFF_HEADS, 1, 2 * tq), F32),
                        pltpu.VMEM((DIFF_HEADS, 1, 2 * tq), F32),
                        pltpu.VMEM((DIFF_HEADS, DIFF_V_DIM + ONES_ROWS, 2 * tq), F32)],
        compiler_params=_cparams(2),
        name="diff_attn",
    )(qt, k, vt, lam_params, sub_gain)


def _gelu_tanh(x):
    return 0.5 * x * (1.0 + jnp.tanh(math.sqrt(2.0 / math.pi) * (x + 0.044715 * (x * x * x))))


def _sigmoid(x):
    return 1.0 / (1.0 + jnp.exp(-x))


def _post_kernel(tok_ref, omem_ref, x_ref, wglu_ref, wout_ref, g_ref, wg_ref, wu_ref,
                 wd_ref, o_ref, *, s5):
    tok = tok_ref[...].T
    if s5:
        h = _gelu_tanh(tok)
        gate = _sigmoid(_dot(h.astype(BF16), wglu_ref[0].astype(BF16)))
        tok = h * gate
    mix = jnp.concatenate([tok.astype(BF16), omem_ref[...].T.astype(BF16)], axis=1)
    x1 = x_ref[...] + _dot(mix, wout_ref[0].astype(BF16))
    h2 = _rms(x1, g_ref[...]).astype(BF16)
    gate = _dot(h2, wg_ref[0])
    up = _dot(h2, wu_ref[0])
    act = (gate * _sigmoid(gate) * up).astype(BF16)
    o_ref[...] = x1 + _dot(act, wd_ref[0])


def _post(tok, o_mem, x, w_glu, w_out, g, wg, wu, wd, *, layer, glu_layer, s5, tm=512):
    n_tok = x.shape[0]
    return pl.pallas_call(
        functools.partial(_post_kernel, s5=s5),
        grid=(n_tok // tm,),
        in_specs=[
            pl.BlockSpec((TOK_WIDTH, tm), lambda i: (0, i)),
            pl.BlockSpec((MEM_WIDTH, tm), lambda i: (0, i)),
            pl.BlockSpec((tm, D_MODEL), lambda i: (i, 0)),
            _layer_spec(w_glu.shape, glu_layer),
            _layer_spec(w_out.shape, layer),
            _const_spec((1, D_MODEL)),
            _layer_spec(wg.shape, layer),
            _layer_spec(wu.shape, layer),
            _layer_spec(wd.shape, layer),
        ],
        out_specs=pl.BlockSpec((tm, D_MODEL), lambda i: (i, 0)),
        out_shape=jax.ShapeDtypeStruct((n_tok, D_MODEL), F32),
        compiler_params=_cparams(1),
        name="post_s5" if s5 else "post",
    )(tok, o_mem, x, w_glu, w_out, g, wg, wu, wd)


def _tile_gain(g, reps):
    return jnp.tile(g.astype(F32), reps)[None, :]


def kernel(x, mem, norm_mix, norm_ffn, norm_mem, w_in, w_out, mem_wk, mem_wv, mem_q_norm,
           mem_k_norm, ffn_w_gate, ffn_w_up, ffn_w_down, ssm_a_re, ssm_a_im, ssm_log_dt,
           ssm_b_re, ssm_b_im, ssm_c_re, ssm_c_im, ssm_d, ssm_w_glu, kv_norm, diff_wk, diff_wv,
           diff_k_norm, diff_q_norm, diff_lambda_q1, diff_lambda_k1, diff_lambda_q2,
           diff_lambda_k2, diff_sub_norm):
    bsz, seq, _ = x.shape
    n_tok = bsz * seq
    xf = x.reshape(n_tok, D_MODEL)

    k_gain = jnp.tile(mem_k_norm.astype(F32), (1, MEM_HEADS))[:, None, :]
    k_mem, vt_mem = _mem_kv(mem, norm_mem[:, None, :], mem_wk, mem_wv, k_gain)

    ffn_gate, ffn_up, ffn_down = (w.astype(BF16) for w in (ffn_w_gate, ffn_w_up, ffn_w_down))

    k_sh = v_sh = None
    for layer in range(DEPTH):
        s5 = layer < N_A_LAYERS
        q_gain = _tile_gain(mem_q_norm[layer], MEM_HEADS).T
        if s5:
            tok_gain = jnp.ones((TOK_WIDTH, 1), F32)
        else:
            tok_gain = _tile_gain(diff_q_norm[layer - N_A_LAYERS], 2 * DIFF_HEADS).T
        z_tok, o_mem = _in_proj(xf, norm_mix[layer][None, :], w_in,
                                tok_gain, q_gain, k_mem, vt_mem,
                                layer=layer, seq=seq, qk_norm=not s5)
        if s5:
            i = layer
            log_dt = ssm_log_dt[i][:, None, None]
            a_row = jnp.tile(jnp.stack([ssm_a_re[i], ssm_a_im[i]], axis=1), (1, 1, 2))
            a_col = jnp.stack([ssm_a_re[i], ssm_a_im[i]], axis=2)
            b_t = jnp.tile(jnp.stack([ssm_b_re[i], ssm_b_im[i]], axis=1).transpose(0, 1, 3, 2),
                           (1, 1, 1, 2))
            c_t = jnp.stack([ssm_c_re[i], ssm_c_im[i]], axis=1).transpose(0, 1, 3, 2)
            m_op, w_op, v_op, ac = _ssm_prep(log_dt, a_row, a_col, b_t, c_t)
            tok = _ssm_scan(z_tok, m_op, w_op, v_op, ac, ssm_d[i][:, None], bsz=bsz)
        else:
            j = layer - N_A_LAYERS
            lambda_init = 0.8 - 0.6 * math.exp(-0.3 * layer)
            lam_params = jnp.stack([diff_lambda_q1[j], diff_lambda_k1[j],
                                    diff_lambda_q2[j], diff_lambda_k2[j]]).astype(F32)
            tok = _diff_attn(z_tok, k_sh, v_sh, lam_params,
                             _tile_gain(diff_sub_norm[j], DIFF_HEADS).T,
                             bsz=bsz, seq=seq, lambda_init=lambda_init)
        xf = _post(tok, o_mem, xf, ssm_w_glu, w_out, norm_ffn[layer][None, :],
                   ffn_gate, ffn_up, ffn_down,
                   layer=layer, glu_layer=layer if s5 else 0, s5=s5)
        if layer == N_A_LAYERS - 1:
            k_sh, v_sh = _shared_kv(xf, kv_norm[None, :], diff_wk, diff_wv,
                                    _tile_gain(diff_k_norm, 2 * DIFF_HEADS))
    return xf.reshape(bsz, seq, D_MODEL)
```

```python
import functools
import math

import jax
import jax.numpy as jnp
from jax import lax
from jax.experimental import pallas as pl
from jax.experimental.pallas import tpu as pltpu

D_MODEL = 1024
DEPTH = 4
CHUNK = 64
N_A_LAYERS = DEPTH // 2
MEM_HEADS = 4
MEM_HEAD_DIM = 64
MEM_WIDTH = MEM_HEADS * MEM_HEAD_DIM
TOK_WIDTH = D_MODEL - MEM_WIDTH
SSM_P = 16
SSM_G = TOK_WIDTH // SSM_P
SSM_N = 64
SSM_C = 64
SSM_K = SSM_P * SSM_C
DIFF_HEAD_DIM = 64
DIFF_HEADS = TOK_WIDTH // (2 * DIFF_HEAD_DIM)
DIFF_V_DIM = 2 * DIFF_HEAD_DIM
D_FF = 2816
EPS = 1e-6

LANES = 128
VMEM_LIMIT = 56 * 1024 * 1024

F32 = jnp.float32
BF16 = jnp.bfloat16


def _cparams(n_grid, vmem=VMEM_LIMIT, flags=None):
    return pltpu.CompilerParams(
        dimension_semantics=("arbitrary",) * n_grid, vmem_limit_bytes=vmem, flags=flags)


def _const_spec(shape):
    nd = len(shape)
    return pl.BlockSpec(shape, lambda *_: (0,) * nd, pipeline_mode=pl.Buffered(1))


def _layer_spec(shape, layer):
    nd = len(shape)
    return pl.BlockSpec((1,) + tuple(shape[1:]), lambda *_: (layer,) + (0,) * (nd - 1),
                        pipeline_mode=pl.Buffered(1))


def _rms(x, g):
    return x * lax.rsqrt(jnp.mean(x * x, axis=-1, keepdims=True) + EPS) * g


def _seg_rms64(x, g):
    rows, width = x.shape
    lane = lax.broadcasted_iota(jnp.int32, (rows, LANES), 1)
    lo = lane < 64
    outs = []
    for v in range(width // LANES):
        xv = x[:, v * LANES:(v + 1) * LANES]
        sq = xv * xv
        s_lo = jnp.sum(jnp.where(lo, sq, 0.0), axis=-1, keepdims=True)
        s_hi = jnp.sum(jnp.where(lo, 0.0, sq), axis=-1, keepdims=True)
        r_lo = lax.rsqrt(s_lo * (1.0 / 64) + EPS)
        r_hi = lax.rsqrt(s_hi * (1.0 / 64) + EPS)
        outs.append(xv * jnp.where(lo, r_lo, r_hi) * g[:, v * LANES:(v + 1) * LANES])
    return jnp.concatenate(outs, axis=1)


def _half_masks(rows, dtype):
    lane = lax.broadcasted_iota(jnp.int32, (rows, LANES), 1)
    lo = (lane < 64).astype(dtype)
    return lo, (1 - lo).astype(dtype)


def _dot(a, b):
    return jnp.dot(a, b, preferred_element_type=F32)


def _dot_nt(a, b):
    return lax.dot_general(a, b, (((1,), (1,)), ((), ())), preferred_element_type=F32)


def _mem_kv_kernel(mem_ref, g_ref, wk_ref, wv_ref, kg_ref, k_ref, vt_ref):
    mn = _rms(mem_ref[0], g_ref[0]).astype(BF16)
    k_ref[0, 0] = _seg_rms64(_dot(mn, wk_ref[0].astype(BF16)), kg_ref[0]).astype(BF16)
    vt_ref[0, 0] = _dot(mn, wv_ref[0].astype(BF16)).T.astype(BF16)


def _mem_kv(mem, norm_mem, wk, wv, k_gain):
    bsz, n_mem, _ = mem.shape
    return pl.pallas_call(
        _mem_kv_kernel,
        grid=(DEPTH, bsz),
        in_specs=[
            pl.BlockSpec((1, n_mem, D_MODEL), lambda l, b: (b, 0, 0)),
            pl.BlockSpec((1, 1, D_MODEL), lambda l, b: (l, 0, 0)),
            pl.BlockSpec((1, D_MODEL, MEM_WIDTH), lambda l, b: (l, 0, 0)),
            pl.BlockSpec((1, D_MODEL, MEM_WIDTH), lambda l, b: (l, 0, 0)),
            pl.BlockSpec((1, 1, MEM_WIDTH), lambda l, b: (l, 0, 0)),
        ],
        out_specs=[pl.BlockSpec((1, 1, n_mem, MEM_WIDTH), lambda l, b: (l, b, 0, 0)),
                   pl.BlockSpec((1, 1, MEM_WIDTH, n_mem), lambda l, b: (l, b, 0, 0))],
        out_shape=[jax.ShapeDtypeStruct((DEPTH, bsz, n_mem, MEM_WIDTH), BF16),
                   jax.ShapeDtypeStruct((DEPTH, bsz, MEM_WIDTH, n_mem), BF16)],
        compiler_params=_cparams(2),
        name="mem_kv",
    )(mem, norm_mem, wk, wv, k_gain)


def _seg_rms_rows(xt, g_col, seg):
    width, cols = xt.shape
    x3 = xt.reshape(width // seg, seg, cols)
    ms = jnp.mean(x3 * x3, axis=1, keepdims=True)
    return (x3 * lax.rsqrt(ms + EPS)).reshape(width, cols) * g_col


ONES_ROWS = 16
SOFTMAX_LOG2E = math.log2(math.e)


def _with_ones_rows(vt):
    return jnp.concatenate([vt, jnp.ones((ONES_ROWS, vt.shape[1]), vt.dtype)], axis=0)


def _mem_attention_t(qt, k, vt):
    tm = qt.shape[1]
    first = lax.broadcasted_iota(jnp.int32, (LANES, tm), 0) < MEM_HEAD_DIM
    scores = []
    for grp in range(MEM_WIDTH // LANES):
        rows = slice(grp * LANES, (grp + 1) * LANES)
        qg = qt[rows, :]
        zero = jnp.zeros_like(qg)
        qm = jnp.concatenate([jnp.where(first, qg, zero), jnp.where(first, zero, qg)], axis=1)
        scores.append(_dot(k[:, rows], qm))
    probs = [jnp.exp2(s - jnp.max(s, axis=0, keepdims=True)).astype(BF16)
             for s in scores]
    outs = []
    for head in range(MEM_HEADS):
        v_rows = _with_ones_rows(vt[head * MEM_HEAD_DIM:(head + 1) * MEM_HEAD_DIM, :])
        o = _dot(v_rows, probs[head // 2][:, (head % 2) * tm:(head % 2 + 1) * tm])
        outs.append(o[:MEM_HEAD_DIM] * (1.0 / o[MEM_HEAD_DIM:MEM_HEAD_DIM + 1]))
    return jnp.concatenate(outs, axis=0)


def _in_proj_kernel(x_ref, g_ref, win_ref, tokg_ref, qg_ref, k_ref, vt_ref,
                    ztok_ref, omem_ref, wt_scr, *, qk_norm):
    @pl.when(pl.program_id(0) == 0)
    def _():
        wt_scr[...] = win_ref[0].T.astype(BF16)

    h = _rms(x_ref[...], g_ref[...]).astype(BF16)
    zt = _dot_nt(wt_scr[...], h)
    z_tok = zt[:TOK_WIDTH]
    if qk_norm:
        z_tok = (_seg_rms_rows(z_tok, tokg_ref[...], DIFF_HEAD_DIM)
                 * (DIFF_HEAD_DIM ** -0.5 * SOFTMAX_LOG2E))
    ztok_ref[...] = z_tok.astype(ztok_ref.dtype)
    q_mem = (_seg_rms_rows(zt[TOK_WIDTH:], qg_ref[...], MEM_HEAD_DIM)
             * (MEM_HEAD_DIM ** -0.5 * SOFTMAX_LOG2E))
    omem_ref[...] = _mem_attention_t(q_mem.astype(BF16), k_ref[0, 0],
                                     vt_ref[0, 0]).astype(omem_ref.dtype)


def _in_proj(x, g, w_in, tok_gain, q_gain, k_mem, vt_mem, *, layer, seq, qk_norm, tm=512):
    n_tok = x.shape[0]
    n_mem = k_mem.shape[2]
    tiles_per_seq = seq // tm
    return pl.pallas_call(
        functools.partial(_in_proj_kernel, qk_norm=qk_norm),
        grid=(n_tok // tm,),
        in_specs=[
            pl.BlockSpec((tm, D_MODEL), lambda i: (i, 0)),
            _const_spec((1, D_MODEL)),
            _layer_spec(w_in.shape, layer),
            _const_spec((TOK_WIDTH, 1)),
            _const_spec((MEM_WIDTH, 1)),
            pl.BlockSpec((1, 1, n_mem, MEM_WIDTH), lambda i: (layer, i // tiles_per_seq, 0, 0)),
            pl.BlockSpec((1, 1, MEM_WIDTH, n_mem), lambda i: (layer, i // tiles_per_seq, 0, 0)),
        ],
        out_specs=[pl.BlockSpec((TOK_WIDTH, tm), lambda i: (0, i)),
                   pl.BlockSpec((MEM_WIDTH, tm), lambda i: (0, i))],
        out_shape=[jax.ShapeDtypeStruct((TOK_WIDTH, n_tok), BF16 if qk_norm else F32),
                   jax.ShapeDtypeStruct((MEM_WIDTH, n_tok), F32)],
        scratch_shapes=[pltpu.VMEM((D_MODEL, D_MODEL), BF16)],
        compiler_params=_cparams(1),
        name="in_proj_q" if qk_norm else "in_proj_u",
    )(x, g, w_in, tok_gain, q_gain, k_mem, vt_mem)


def _cpow(ar, ai, dt, k):
    mag = jnp.exp(k * dt * ar)
    ang = k * dt * ai
    return mag * jnp.cos(ang), mag * jnp.sin(ang)


def _zoh_coef(ar, ai, abr, abi):
    den = ar * ar + ai * ai
    nr = abr - 1.0
    return (nr * ar + abi * ai) / den, (abi * ar - nr * ai) / den


def _ssm_prep_kernel(ldt_ref, arow_ref, acol_ref, bt_ref, ct_ref, m_ref, w_ref, v_ref, ac_ref):
    dt = jnp.exp(ldt_ref[0])
    first = lax.broadcasted_iota(jnp.int32, (1, LANES), 1) < SSM_N
    ar_r, ai_r = arow_ref[0, 0:1, :], arow_ref[0, 1:2, :]
    ar_c, ai_c = acol_ref[0, :, 0:1], acol_ref[0, :, 1:2]

    s_row = lax.broadcasted_iota(jnp.int32, (SSM_C, LANES), 0).astype(F32)
    e_re, e_im = _cpow(ar_r, ai_r, dt, float(SSM_C - 1) - s_row)
    ab_re, ab_im = e_re[SSM_C - 2:SSM_C - 1], e_im[SSM_C - 2:SSM_C - 1]
    cf_re, cf_im = _zoh_coef(ar_r, ai_r, ab_re, ab_im)

    pc_re = e_re[0:1] * ab_re - e_im[0:1] * ab_im
    pc_im = e_re[0:1] * ab_im + e_im[0:1] * ab_re
    ac_ref[0, 0:1, :] = pc_re
    ac_ref[0, 1:2, :] = jnp.where(first, -pc_im, pc_im)

    bre, bim = bt_ref[0, 0], bt_ref[0, 1]
    bb_re = cf_re * bre - cf_im * bim
    bb_im = cf_re * bim + cf_im * bre

    for p in range(SSM_P):
        br, bi = bb_re[p:p + 1, :], bb_im[p:p + 1, :]
        w_ref[0, p * SSM_C:(p + 1) * SSM_C, :] = jnp.where(
            first, br * e_re - bi * e_im, br * e_im + bi * e_re).astype(BF16)

    lane_n = lax.broadcasted_iota(jnp.int32, (SSM_N, LANES), 1)
    tau = (lane_n % SSM_C).astype(F32)
    p0_re, p0_im = _cpow(ar_c, ai_c, dt, tau)
    ab_re_c, ab_im_c = p0_re[:, 1:2], p0_im[:, 1:2]
    p1_re = p0_re * ab_re_c - p0_im * ab_im_c
    p1_im = p0_re * ab_im_c + p0_im * ab_re_c
    reps = SSM_K // LANES
    p0_re, p0_im = jnp.concatenate([p0_re] * reps, 1), jnp.concatenate([p0_im] * reps, 1)
    p1_re, p1_im = jnp.concatenate([p1_re] * reps, 1), jnp.concatenate([p1_im] * reps, 1)

    sel = (lax.broadcasted_iota(jnp.int32, (SSM_P, SSM_K), 1) // SSM_C
           == lax.broadcasted_iota(jnp.int32, (SSM_P, SSM_K), 0)).astype(BF16)

    def expand(ct):
        hi = ct.astype(BF16)
        rest = ct - hi.astype(F32)
        mid = rest.astype(BF16)
        lo = (rest - mid.astype(F32)).astype(BF16)
        return _dot(hi, sel) + _dot(mid, sel) + _dot(lo, sel)

    c_re, c_im = expand(ct_ref[0, 0]), expand(ct_ref[0, 1])

    v_ref[0, 0:SSM_N, :] = (c_re * p1_re - c_im * p1_im).astype(BF16)
    v_ref[0, SSM_N:2 * SSM_N, :] = (-(c_re * p1_im + c_im * p1_re)).astype(BF16)

    ce = jnp.concatenate([c_re * p0_re - c_im * p0_im, -(c_re * p0_im + c_im * p0_re)], axis=0)
    kt = jnp.dot(jnp.where(first, bb_re, bb_im), ce, precision=lax.Precision.HIGHEST,
                 preferred_element_type=F32)

    row = lax.broadcasted_iota(jnp.int32, (SSM_C, SSM_K), 0)
    lane_t = lax.broadcasted_iota(jnp.int32, (SSM_C, SSM_K), 1) % SSM_C
    causal = lane_t >= row
    for p in range(SSM_P):
        base = jnp.broadcast_to(kt[p:p + 1, :], (SSM_C, SSM_K))
        shifted = pltpu.roll(base, 0, 1, stride=1, stride_axis=0)
        m_ref[0, p * SSM_C:(p + 1) * SSM_C, :] = jnp.where(causal, shifted, 0.0).astype(BF16)


def _ssm_prep(log_dt, a_row, a_col, b_t, c_t):
    g = log_dt.shape[0]
    return pl.pallas_call(
        _ssm_prep_kernel,
        grid=(g,),
        in_specs=[
            pl.BlockSpec((1, 1, 1), lambda i: (i, 0, 0)),
            pl.BlockSpec((1, 2, 2 * SSM_N), lambda i: (i, 0, 0)),
            pl.BlockSpec((1, SSM_N, 2), lambda i: (i, 0, 0)),
            pl.BlockSpec((1, 2, SSM_P, 2 * SSM_N), lambda i: (i, 0, 0, 0)),
            pl.BlockSpec((1, 2, SSM_N, SSM_P), lambda i: (i, 0, 0, 0)),
        ],
        out_specs=[
            pl.BlockSpec((1, SSM_K, SSM_K), lambda i: (i, 0, 0)),
            pl.BlockSpec((1, SSM_K, 2 * SSM_N), lambda i: (i, 0, 0)),
            pl.BlockSpec((1, 2 * SSM_N, SSM_K), lambda i: (i, 0, 0)),
            pl.BlockSpec((1, 2, 2 * SSM_N), lambda i: (i, 0, 0)),
        ],
        out_shape=[
            jax.ShapeDtypeStruct((g, SSM_K, SSM_K), BF16),
            jax.ShapeDtypeStruct((g, SSM_K, 2 * SSM_N), BF16),
            jax.ShapeDtypeStruct((g, 2 * SSM_N, SSM_K), BF16),
            jax.ShapeDtypeStruct((g, 2, 2 * SSM_N), F32),
        ],
        compiler_params=_cparams(1),
        name="ssm_prep",
    )(log_dt, a_row, a_col, b_t, c_t)


SUBLANES = 8


def _transpose8(tiles):
    sub = lax.broadcasted_iota(jnp.int32, tiles[0].shape, 0)
    t = list(tiles)
    for d in (4, 2, 1):
        clear = (sub & d) == 0
        for i0 in range(SUBLANES):
            if i0 & d:
                continue
            lo, hi = t[i0], t[i0 + d]
            t[i0] = jnp.where(clear, lo, pltpu.roll(hi, d, 0))
            t[i0 + d] = jnp.where(clear, pltpu.roll(lo, SUBLANES - d, 0), hi)
    return t


def _rows_to_lane_blocks(x):
    n_rows, width = x.shape
    n = width // LANES
    out_rows = []
    for k in range(n // SUBLANES):
        pieces = []
        for b in range(n_rows // SUBLANES):
            xb = x[b * SUBLANES:(b + 1) * SUBLANES]
            pieces += _transpose8([xb[:, (SUBLANES * k + i) * LANES:(SUBLANES * k + i + 1) * LANES]
                                   for i in range(SUBLANES)])
        out_rows.append(jnp.concatenate(pieces, axis=1))
    return jnp.concatenate(out_rows, axis=0)


def _lane_blocks_to_rows(y, n_rows):
    n = y.shape[0]
    out_rows = []
    for b in range(n_rows // SUBLANES):
        pieces = []
        for k in range(n // SUBLANES):
            yk = y[k * SUBLANES:(k + 1) * SUBLANES]
            pieces += _transpose8([yk[:, (SUBLANES * b + j) * LANES:(SUBLANES * b + j + 1) * LANES]
                                   for j in range(SUBLANES)])
        out_rows.append(jnp.concatenate(pieces, axis=1))
    return jnp.concatenate(out_rows, axis=0)

def _ssm_scan_kernel(u_ref, m_ref, w_ref, v_ref, ac_ref, d_ref, y_ref, z_scr, zs_scr, sp_scr,
                     *, bsz, n_chunks):
    pairs = n_chunks // 2
    n_pairs = bsz * pairs
    ut = u_ref[...]
    u2 = _rows_to_lane_blocks(ut)
    low = lax.broadcasted_iota(jnp.int32, (n_pairs, LANES), 1) < SSM_C

    def col(a, j):
        return a[:, j * LANES:(j + 1) * LANES]

    def swap(a):
        return pltpu.roll(a, SSM_C, 1)

    even = [jnp.where(low, col(u2, 2 * k), swap(col(u2, 2 * k + 1))) for k in range(SSM_P // 2)]
    odd = [jnp.where(low, swap(col(u2, 2 * k)), col(u2, 2 * k + 1)) for k in range(SSM_P // 2)]
    ub = jnp.concatenate([jnp.concatenate(even, axis=1), jnp.concatenate(odd, axis=1)],
                         axis=0).astype(BF16)
    y = _dot(ub, m_ref[0])
    z = _dot(ub, w_ref[0])
    z_scr[...] = z
    zs_scr[...] = pltpu.roll(z, SSM_N, 1)

    a1 = ac_ref[0, 0:1, :]
    a2 = ac_ref[0, 1:2, :]
    s = jnp.zeros((bsz, 2 * SSM_N), F32)
    ss = jnp.zeros((bsz, 2 * SSM_N), F32)
    for c in range(n_chunks):
        rows = pl.ds((c % 2) * n_pairs + c // 2, bsz, stride=pairs)
        sp_scr[rows, :] = s
        if c + 1 < n_chunks:
            s, ss = (a1 * s + a2 * ss + z_scr[rows, :],
                     a1 * ss - a2 * s + zs_scr[rows, :])
    y = y + _dot(sp_scr[...].astype(BF16), v_ref[0])
    y_even, y_odd = y[:n_pairs], y[n_pairs:]
    cols = []
    for k in range(SSM_P // 2):
        ce, co = col(y_even, k), col(y_odd, k)
        cols += [jnp.where(low, ce, swap(co)), jnp.where(low, swap(ce), co)]
    y2 = jnp.concatenate(cols, axis=1)
    yt = _lane_blocks_to_rows(y2, SSM_P)
    y_ref[...] = yt + d_ref[...] * ut


def _ssm_scan(zt, m, w, v, ac, d_col, *, bsz):
    n_tok = zt.shape[1]
    n_rows = n_tok // SSM_C
    n_chunks = n_rows // bsz
    kern = functools.partial(_ssm_scan_kernel, bsz=bsz, n_chunks=n_chunks)
    return pl.pallas_call(
        kern,
        grid=(SSM_G,),
        in_specs=[
            pl.BlockSpec((SSM_P, n_tok), lambda i: (i, 0)),
            pl.BlockSpec((1, SSM_K, SSM_K), lambda i: (i, 0, 0)),
            pl.BlockSpec((1, SSM_K, 2 * SSM_N), lambda i: (i, 0, 0)),
            pl.BlockSpec((1, 2 * SSM_N, SSM_K), lambda i: (i, 0, 0)),
            pl.BlockSpec((1, 2, 2 * SSM_N), lambda i: (i, 0, 0)),
            pl.BlockSpec((SSM_P, 1), lambda i: (i, 0)),
        ],
        out_specs=pl.BlockSpec((SSM_P, n_tok), lambda i: (i, 0)),
        out_shape=jax.ShapeDtypeStruct((TOK_WIDTH, n_tok), F32),
        scratch_shapes=[pltpu.VMEM((n_rows, 2 * SSM_N), F32)] * 3,
        compiler_params=_cparams(1),
        name="ssm_scan",
    )(zt, m, w, v, ac, d_col)


ATTN_BLOCK = 256
KV_SUB = 2


def _shared_kv_kernel(x_ref, g_ref, wk_ref, wv_ref, kg_ref, k_ref, vt_ref):
    h = _rms(x_ref[...], g_ref[...]).astype(BF16)
    k_ref[...] = _seg_rms64(_dot(h, wk_ref[...].astype(BF16)), kg_ref[...]).astype(BF16)
    v = _dot(h, wv_ref[...].astype(BF16))
    for j in range(vt_ref.shape[0]):
        vt_ref[j] = v[j * ATTN_BLOCK:(j + 1) * ATTN_BLOCK].T.astype(BF16)


def _shared_kv(x, g, wk, wv, k_gain, tm=512):
    n_tok = x.shape[0]
    sub = tm // ATTN_BLOCK
    return pl.pallas_call(
        _shared_kv_kernel,
        grid=(n_tok // tm,),
        in_specs=[
            pl.BlockSpec((tm, D_MODEL), lambda i: (i, 0)),
            _const_spec((1, D_MODEL)),
            _const_spec(wk.shape),
            _const_spec(wv.shape),
            _const_spec((1, TOK_WIDTH)),
        ],
        out_specs=[pl.BlockSpec((tm, TOK_WIDTH), lambda i: (i, 0)),
                   pl.BlockSpec((sub, TOK_WIDTH, ATTN_BLOCK), lambda i: (i, 0, 0))],
        out_shape=[jax.ShapeDtypeStruct((n_tok, TOK_WIDTH), BF16),
                   jax.ShapeDtypeStruct((n_tok // ATTN_BLOCK, TOK_WIDTH, ATTN_BLOCK), BF16)],
        compiler_params=_cparams(1),
        name="shared_kv",
    )(x, g, wk, wv, k_gain)


N_STREAMS = 2 * DIFF_HEADS


def _diff_attn_kernel(qt_ref, k_ref, vt_ref, lam_ref, subg_ref, o_ref, qm_scr, sc_scr,
                      bm_scr, m_scr, acc_scr, *, lambda_init):
    tq = ATTN_BLOCK
    qi = pl.program_id(1)
    lp = lam_ref[...]
    lam = (jnp.exp(jnp.sum(lp[0:1] * lp[1:2], axis=-1, keepdims=True))
           - jnp.exp(jnp.sum(lp[2:3] * lp[3:4], axis=-1, keepdims=True)) + lambda_init)
    key_chunk = lax.broadcasted_iota(jnp.int32, (tq, tq), 0) // CHUNK
    qry_chunk = lax.broadcasted_iota(jnp.int32, (tq, tq), 1) // CHUNK
    diag_mask = key_chunk <= qry_chunk
    first_half = lax.broadcasted_iota(jnp.int32, (LANES, tq), 0) < DIFF_HEAD_DIM
    n_grp = DIFF_HEADS // 2

    for kg in range(2 * n_grp):
        qg = qt_ref[kg * LANES:(kg + 1) * LANES, :]
        zero = jnp.zeros_like(qg)
        qm_scr[kg, :, 0:tq] = jnp.where(first_half, qg, zero)
        qm_scr[kg, :, tq:2 * tq] = jnp.where(first_half, zero, qg)

    def block(j, diagonal, n_sub=1):
        nk = n_sub * tq
        rows = pl.ds(pl.multiple_of(j * tq, tq), nk)
        for kg in range(2 * n_grp):
            kb = k_ref[rows, kg * LANES:(kg + 1) * LANES]
            st = _dot(kb, qm_scr[kg])
            if diagonal:
                st = jnp.where(jnp.concatenate([diag_mask, diag_mask], axis=1), st, -jnp.inf)
            sc_scr[kg, 0:nk, :] = st
            bm_scr[kg] = jnp.max(st, axis=0, keepdims=True)
        for kg in range(2 * n_grp):
            g = kg % n_grp
            if diagonal:
                m_new = bm_scr[kg]
            else:
                m_old = m_scr[kg]
                m_new = jnp.maximum(m_old, bm_scr[kg])
                alpha = jnp.exp2(m_old - m_new)
            pb = jnp.exp2(sc_scr[kg, 0:nk, :] - m_new).astype(BF16)
            m_scr[kg] = m_new

            def v_rows(h):
                vt = [vt_ref[j + i, h * DIFF_V_DIM:(h + 1) * DIFF_V_DIM, :] for i in range(n_sub)]
                return _with_ones_rows(vt[0] if n_sub == 1 else jnp.concatenate(vt, axis=1))

            pv = jnp.concatenate(
                [_dot(v_rows(h), pb[:, hh * tq:(hh + 1) * tq])
                 for hh, h in enumerate((2 * g, 2 * g + 1))], axis=1)
            acc_scr[kg] = pv if diagonal else alpha * acc_scr[kg] + pv

    def body(t, carry):
        block(KV_SUB * t, False, KV_SUB)
        return carry

    block(qi, True)
    lax.fori_loop(0, qi // KV_SUB, body, 0)
    for r in range(1, KV_SUB):
        @pl.when(qi % KV_SUB >= r)
        def _():
            block(qi - qi % KV_SUB + r - 1, False)

    for head in range(DIFF_HEADS):
        rows = slice(head * DIFF_V_DIM, (head + 1) * DIFF_V_DIM)
        q_cols = slice((head % 2) * tq, (head % 2 + 1) * tq)
        on = []
        for comp in range(2):
            acc = acc_scr[comp * n_grp + head // 2]
            on.append(acc[0:DIFF_V_DIM, q_cols] * (1.0 / acc[DIFF_V_DIM:DIFF_V_DIM + 1, q_cols]))
        o = on[0] - lam * on[1]
        o = o * lax.rsqrt(jnp.mean(o * o, axis=0, keepdims=True) + EPS)
        o_ref[rows, :] = (o * subg_ref[rows, :] * (1.0 - lambda_init)).astype(o_ref.dtype)


def _diff_attn(qt, k, vt, lam_params, sub_gain, *, bsz, seq, lambda_init):
    tq = ATTN_BLOCK
    n_q = seq // tq
    return pl.pallas_call(
        functools.partial(_diff_attn_kernel, lambda_init=lambda_init),
        grid=(bsz, n_q),
        in_specs=[
            pl.BlockSpec((TOK_WIDTH, tq), lambda b, i: (0, b * n_q + i)),
            pl.BlockSpec((seq, TOK_WIDTH), lambda b, i: (b, 0)),
            pl.BlockSpec((n_q, TOK_WIDTH, tq), lambda b, i: (b, 0, 0)),
            pl.BlockSpec((4, DIFF_HEAD_DIM), lambda b, i: (0, 0)),
            pl.BlockSpec((TOK_WIDTH, 1), lambda b, i: (0, 0)),
        ],
        out_specs=pl.BlockSpec((TOK_WIDTH, tq), lambda b, i: (0, b * n_q + i)),
        out_shape=jax.ShapeDtypeStruct((TOK_WIDTH, bsz * seq), F32),
        scratch_shapes=[pltpu.VMEM((DIFF_HEADS, LANES, 2 * tq), BF16),
                        pltpu.VMEM((DIFF_HEADS, KV_SUB * tq, 2 * tq), F32),
                        pltpu.VMEM((DIFF_HEADS, 1, 2 * tq), F32),
                        pltpu.VMEM((DIFF_HEADS, 1, 2 * tq), F32),
                        pltpu.VMEM((DIFF_HEADS, DIFF_V_DIM + ONES_ROWS, 2 * tq), F32)],
        compiler_params=_cparams(2),
        name="diff_attn",
    )(qt, k, vt, lam_params, sub_gain)


def _gelu_tanh(x):
    return 0.5 * x * (1.0 + jnp.tanh(math.sqrt(2.0 / math.pi) * (x + 0.044715 * (x * x * x))))


def _sigmoid(x):
    return 1.0 / (1.0 + jnp.exp(-x))


def _post_kernel(tok_ref, omem_ref, x_ref, wglu_ref, wout_ref, g_ref, wg_ref, wu_ref,
                 wd_ref, o_ref, *, s5):
    tok = tok_ref[...].T
    if s5:
        h = _gelu_tanh(tok)
        gate = _sigmoid(_dot(h.astype(BF16), wglu_ref[0].astype(BF16)))
        tok = h * gate
    mix = jnp.concatenate([tok.astype(BF16), omem_ref[...].T.astype(BF16)], axis=1)
    x1 = x_ref[...] + _dot(mix, wout_ref[0].astype(BF16))
    h2 = _rms(x1, g_ref[...]).astype(BF16)
    gate = _dot(h2, wg_ref[0])
    up = _dot(h2, wu_ref[0])
    act = (gate * _sigmoid(gate) * up).astype(BF16)
    o_ref[...] = x1 + _dot(act, wd_ref[0])


def _post(tok, o_mem, x, w_glu, w_out, g, wg, wu, wd, *, layer, glu_layer, s5, tm=512):
    n_tok = x.shape[0]
    return pl.pallas_call(
        functools.partial(_post_kernel, s5=s5),
        grid=(n_tok // tm,),
        in_specs=[
            pl.BlockSpec((TOK_WIDTH, tm), lambda i: (0, i)),
            pl.BlockSpec((MEM_WIDTH, tm), lambda i: (0, i)),
            pl.BlockSpec((tm, D_MODEL), lambda i: (i, 0)),
            _layer_spec(w_glu.shape, glu_layer),
            _layer_spec(w_out.shape, layer),
            _const_spec((1, D_MODEL)),
            _layer_spec(wg.shape, layer),
            _layer_spec(wu.shape, layer),
            _layer_spec(wd.shape, layer),
        ],
        out_specs=pl.BlockSpec((tm, D_MODEL), lambda i: (i, 0)),
        out_shape=jax.ShapeDtypeStruct((n_tok, D_MODEL), F32),
        compiler_params=_cparams(1),
        name="post_s5" if s5 else "post",
    )(tok, o_mem, x, w_glu, w_out, g, wg, wu, wd)


def _tile_gain(g, reps):
    return jnp.tile(g.astype(F32), reps)[None, :]


def kernel(x, mem, norm_mix, norm_ffn, norm_mem, w_in, w_out, mem_wk, mem_wv, mem_q_norm,
           mem_k_norm, ffn_w_gate, ffn_w_up, ffn_w_down, ssm_a_re, ssm_a_im, ssm_log_dt,
           ssm_b_re, ssm_b_im, ssm_c_re, ssm_c_im, ssm_d, ssm_w_glu, kv_norm, diff_wk, diff_wv,
           diff_k_norm, diff_q_norm, diff_lambda_q1, diff_lambda_k1, diff_lambda_q2,
           diff_lambda_k2, diff_sub_norm):
    bsz, seq, _ = x.shape
    n_tok = bsz * seq
    xf = x.reshape(n_tok, D_MODEL)

    k_gain = jnp.tile(mem_k_norm.astype(F32), (1, MEM_HEADS))[:, None, :]
    k_mem, vt_mem = _mem_kv(mem, norm_mem[:, None, :], mem_wk, mem_wv, k_gain)

    ffn_gate, ffn_up, ffn_down = (w.astype(BF16) for w in (ffn_w_gate, ffn_w_up, ffn_w_down))

    k_sh = v_sh = None
    for layer in range(DEPTH):
        s5 = layer < N_A_LAYERS
        q_gain = _tile_gain(mem_q_norm[layer], MEM_HEADS).T
        if s5:
            tok_gain = jnp.ones((TOK_WIDTH, 1), F32)
        else:
            tok_gain = _tile_gain(diff_q_norm[layer - N_A_LAYERS], 2 * DIFF_HEADS).T
        z_tok, o_mem = _in_proj(xf, norm_mix[layer][None, :], w_in,
                                tok_gain, q_gain, k_mem, vt_mem,
                                layer=layer, seq=seq, qk_norm=not s5)
        if s5:
            i = layer
            log_dt = ssm_log_dt[i][:, None, None]
            a_row = jnp.tile(jnp.stack([ssm_a_re[i], ssm_a_im[i]], axis=1), (1, 1, 2))
            a_col = jnp.stack([ssm_a_re[i], ssm_a_im[i]], axis=2)
            b_t = jnp.tile(jnp.stack([ssm_b_re[i], ssm_b_im[i]], axis=1).transpose(0, 1, 3, 2),
                           (1, 1, 1, 2))
            c_t = jnp.stack([ssm_c_re[i], ssm_c_im[i]], axis=1).transpose(0, 1, 3, 2)
            m_op, w_op, v_op, ac = _ssm_prep(log_dt, a_row, a_col, b_t, c_t)
            tok = _ssm_scan(z_tok, m_op, w_op, v_op, ac, ssm_d[i][:, None], bsz=bsz)
        else:
            j = layer - N_A_LAYERS
            lambda_init = 0.8 - 0.6 * math.exp(-0.3 * layer)
            lam_params = jnp.stack([diff_lambda_q1[j], diff_lambda_k1[j],
                                    diff_lambda_q2[j], diff_lambda_k2[j]]).astype(F32)
            tok = _diff_attn(z_tok, k_sh, v_sh, lam_params,
                             _tile_gain(diff_sub_norm[j], DIFF_HEADS).T,
                             bsz=bsz, seq=seq, lambda_init=lambda_init)
        xf = _post(tok, o_mem, xf, ssm_w_glu, w_out, norm_ffn[layer][None, :],
                   ffn_gate, ffn_up, ffn_down,
                   layer=layer, glu_layer=layer if s5 else 0, s5=s5)
        if layer == N_A_LAYERS - 1:
            k_sh, v_sh = _shared_kv(xf, kv_norm[None, :], diff_wk, diff_wv,
                                    _tile_gain(diff_k_norm, 2 * DIFF_HEADS))
    return xf.reshape(bsz, seq, D_MODEL)
```

```python
import functools
import math

import jax
import jax.numpy as jnp
from jax import lax
from jax.experimental import pallas as pl
from jax.experimental.pallas import tpu as pltpu

D_MODEL = 1024
DEPTH = 4
CHUNK = 64
N_A_LAYERS = DEPTH // 2
MEM_HEADS = 4
MEM_HEAD_DIM = 64
MEM_WIDTH = MEM_HEADS * MEM_HEAD_DIM
TOK_WIDTH = D_MODEL - MEM_WIDTH
SSM_P = 16
SSM_G = TOK_WIDTH // SSM_P
SSM_N = 64
SSM_C = 64
SSM_K = SSM_P * SSM_C
DIFF_HEAD_DIM = 64
DIFF_HEADS = TOK_WIDTH // (2 * DIFF_HEAD_DIM)
DIFF_V_DIM = 2 * DIFF_HEAD_DIM
D_FF = 2816
EPS = 1e-6

LANES = 128
VMEM_LIMIT = 56 * 1024 * 1024

F32 = jnp.float32
BF16 = jnp.bfloat16


def _cparams(n_grid, vmem=VMEM_LIMIT, flags=None):
    return pltpu.CompilerParams(
        dimension_semantics=("arbitrary",) * n_grid, vmem_limit_bytes=vmem, flags=flags)


def _const_spec(shape):
    nd = len(shape)
    return pl.BlockSpec(shape, lambda *_: (0,) * nd, pipeline_mode=pl.Buffered(1))


def _layer_spec(shape, layer):
    nd = len(shape)
    return pl.BlockSpec((1,) + tuple(shape[1:]), lambda *_: (layer,) + (0,) * (nd - 1),
                        pipeline_mode=pl.Buffered(1))


def _rms(x, g):
    return x * lax.rsqrt(jnp.mean(x * x, axis=-1, keepdims=True) + EPS) * g


def _seg_rms64(x, g):
    rows, width = x.shape
    lane = lax.broadcasted_iota(jnp.int32, (rows, LANES), 1)
    lo = lane < 64
    outs = []
    for v in range(width // LANES):
        xv = x[:, v * LANES:(v + 1) * LANES]
        sq = xv * xv
        s_lo = jnp.sum(jnp.where(lo, sq, 0.0), axis=-1, keepdims=True)
        s_hi = jnp.sum(jnp.where(lo, 0.0, sq), axis=-1, keepdims=True)
        r_lo = lax.rsqrt(s_lo * (1.0 / 64) + EPS)
        r_hi = lax.rsqrt(s_hi * (1.0 / 64) + EPS)
        outs.append(xv * jnp.where(lo, r_lo, r_hi) * g[:, v * LANES:(v + 1) * LANES])
    return jnp.concatenate(outs, axis=1)


def _half_masks(rows, dtype):
    lane = lax.broadcasted_iota(jnp.int32, (rows, LANES), 1)
    lo = (lane < 64).astype(dtype)
    return lo, (1 - lo).astype(dtype)


def _dot(a, b):
    return jnp.dot(a, b, preferred_element_type=F32)


def _dot_nt(a, b):
    return lax.dot_general(a, b, (((1,), (1,)), ((), ())), preferred_element_type=F32)


def _mem_kv_kernel(mem_ref, g_ref, wk_ref, wv_ref, kg_ref, k_ref, vt_ref):
    bsz, n_mem, _ = mem_ref.shape
    mn = _rms(mem_ref[...].reshape(bsz * n_mem, D_MODEL), g_ref[0]).astype(BF16)
    k = _seg_rms64(_dot(mn, wk_ref[0].astype(BF16)), kg_ref[0])
    k_ref[0] = k.reshape(bsz, n_mem, MEM_WIDTH).astype(BF16)
    v = _dot(mn, wv_ref[0].astype(BF16))
    for b in range(bsz):
        vt_ref[0, b] = v[b * n_mem:(b + 1) * n_mem].T.astype(BF16)


def _mem_kv(mem, norm_mem, wk, wv, k_gain):
    bsz, n_mem, _ = mem.shape
    return pl.pallas_call(
        _mem_kv_kernel,
        grid=(DEPTH,),
        in_specs=[
            _const_spec(mem.shape),
            pl.BlockSpec((1, 1, D_MODEL), lambda l: (l, 0, 0)),
            pl.BlockSpec((1, D_MODEL, MEM_WIDTH), lambda l: (l, 0, 0)),
            pl.BlockSpec((1, D_MODEL, MEM_WIDTH), lambda l: (l, 0, 0)),
            pl.BlockSpec((1, 1, MEM_WIDTH), lambda l: (l, 0, 0)),
        ],
        out_specs=[pl.BlockSpec((1, bsz, n_mem, MEM_WIDTH), lambda l: (l, 0, 0, 0)),
                   pl.BlockSpec((1, bsz, MEM_WIDTH, n_mem), lambda l: (l, 0, 0, 0))],
        out_shape=[jax.ShapeDtypeStruct((DEPTH, bsz, n_mem, MEM_WIDTH), BF16),
                   jax.ShapeDtypeStruct((DEPTH, bsz, MEM_WIDTH, n_mem), BF16)],
        compiler_params=_cparams(1),
        name="mem_kv",
    )(mem, norm_mem, wk, wv, k_gain)


def _seg_rms_rows(xt, g_col, seg):
    width, cols = xt.shape
    x3 = xt.reshape(width // seg, seg, cols)
    ms = jnp.mean(x3 * x3, axis=1, keepdims=True)
    return (x3 * lax.rsqrt(ms + EPS)).reshape(width, cols) * g_col


ONES_ROWS = 16
SOFTMAX_LOG2E = math.log2(math.e)


def _with_ones_rows(vt):
    return jnp.concatenate([vt, jnp.ones((ONES_ROWS, vt.shape[1]), vt.dtype)], axis=0)


def _mem_attention_t(qt, k, vt):
    tm = qt.shape[1]
    first = lax.broadcasted_iota(jnp.int32, (LANES, tm), 0) < MEM_HEAD_DIM
    scores = []
    for grp in range(MEM_WIDTH // LANES):
        rows = slice(grp * LANES, (grp + 1) * LANES)
        qg = qt[rows, :]
        zero = jnp.zeros_like(qg)
        qm = jnp.concatenate([jnp.where(first, qg, zero), jnp.where(first, zero, qg)], axis=1)
        scores.append(_dot(k[:, rows], qm))
    probs = [jnp.exp2(s - jnp.max(s, axis=0, keepdims=True)).astype(BF16)
             for s in scores]
    outs = []
    for head in range(MEM_HEADS):
        v_rows = _with_ones_rows(vt[head * MEM_HEAD_DIM:(head + 1) * MEM_HEAD_DIM, :])
        o = _dot(v_rows, probs[head // 2][:, (head % 2) * tm:(head % 2 + 1) * tm])
        outs.append(o[:MEM_HEAD_DIM] * (1.0 / o[MEM_HEAD_DIM:MEM_HEAD_DIM + 1]))
    return jnp.concatenate(outs, axis=0)


def _in_proj_kernel(x_ref, g_ref, win_ref, tokg_ref, qg_ref, k_ref, vt_ref,
                    ztok_ref, omem_ref, wt_scr, *, qk_norm):
    @pl.when(pl.program_id(0) == 0)
    def _():
        wt_scr[...] = win_ref[0].T.astype(BF16)

    h = _rms(x_ref[...], g_ref[...]).astype(BF16)
    zt = _dot_nt(wt_scr[...], h)
    z_tok = zt[:TOK_WIDTH]
    if qk_norm:
        z_tok = (_seg_rms_rows(z_tok, tokg_ref[...], DIFF_HEAD_DIM)
                 * (DIFF_HEAD_DIM ** -0.5 * SOFTMAX_LOG2E))
    ztok_ref[...] = z_tok.astype(ztok_ref.dtype)
    q_mem = (_seg_rms_rows(zt[TOK_WIDTH:], qg_ref[...], MEM_HEAD_DIM)
             * (MEM_HEAD_DIM ** -0.5 * SOFTMAX_LOG2E))
    omem_ref[...] = _mem_attention_t(q_mem.astype(BF16), k_ref[0, 0],
                                     vt_ref[0, 0]).astype(omem_ref.dtype)


def _in_proj(x, g, w_in, tok_gain, q_gain, k_mem, vt_mem, *, layer, seq, qk_norm, tm=512):
    n_tok = x.shape[0]
    n_mem = k_mem.shape[2]
    tiles_per_seq = seq // tm
    return pl.pallas_call(
        functools.partial(_in_proj_kernel, qk_norm=qk_norm),
        grid=(n_tok // tm,),
        in_specs=[
            pl.BlockSpec((tm, D_MODEL), lambda i: (i, 0)),
            _const_spec((1, D_MODEL)),
            _layer_spec(w_in.shape, layer),
            _const_spec((TOK_WIDTH, 1)),
            _const_spec((MEM_WIDTH, 1)),
            pl.BlockSpec((1, 1, n_mem, MEM_WIDTH), lambda i: (layer, i // tiles_per_seq, 0, 0)),
            pl.BlockSpec((1, 1, MEM_WIDTH, n_mem), lambda i: (layer, i // tiles_per_seq, 0, 0)),
        ],
        out_specs=[pl.BlockSpec((TOK_WIDTH, tm), lambda i: (0, i)),
                   pl.BlockSpec((MEM_WIDTH, tm), lambda i: (0, i))],
        out_shape=[jax.ShapeDtypeStruct((TOK_WIDTH, n_tok), BF16 if qk_norm else F32),
                   jax.ShapeDtypeStruct((MEM_WIDTH, n_tok), F32)],
        scratch_shapes=[pltpu.VMEM((D_MODEL, D_MODEL), BF16)],
        compiler_params=_cparams(1),
        name="in_proj_q" if qk_norm else "in_proj_u",
    )(x, g, w_in, tok_gain, q_gain, k_mem, vt_mem)


def _cpow(ar, ai, dt, k):
    mag = jnp.exp(k * dt * ar)
    ang = k * dt * ai
    return mag * jnp.cos(ang), mag * jnp.sin(ang)


def _zoh_coef(ar, ai, abr, abi):
    den = ar * ar + ai * ai
    nr = abr - 1.0
    return (nr * ar + abi * ai) / den, (abi * ar - nr * ai) / den


def _ssm_prep_kernel(ldt_ref, arow_ref, acol_ref, bt_ref, ct_ref, m_ref, w_ref, v_ref, ac_ref):
    dt = jnp.exp(ldt_ref[0])
    first = lax.broadcasted_iota(jnp.int32, (1, LANES), 1) < SSM_N
    ar_r, ai_r = arow_ref[0, 0:1, :], arow_ref[0, 1:2, :]
    ar_c, ai_c = acol_ref[0, :, 0:1], acol_ref[0, :, 1:2]

    s_row = lax.broadcasted_iota(jnp.int32, (SSM_C, LANES), 0).astype(F32)
    e_re, e_im = _cpow(ar_r, ai_r, dt, float(SSM_C - 1) - s_row)
    ab_re, ab_im = e_re[SSM_C - 2:SSM_C - 1], e_im[SSM_C - 2:SSM_C - 1]
    cf_re, cf_im = _zoh_coef(ar_r, ai_r, ab_re, ab_im)

    pc_re = e_re[0:1] * ab_re - e_im[0:1] * ab_im
    pc_im = e_re[0:1] * ab_im + e_im[0:1] * ab_re
    ac_ref[0, 0:1, :] = pc_re
    ac_ref[0, 1:2, :] = jnp.where(first, -pc_im, pc_im)

    bre, bim = bt_ref[0, 0], bt_ref[0, 1]
    bb_re = cf_re * bre - cf_im * bim
    bb_im = cf_re * bim + cf_im * bre

    for p in range(SSM_P):
        br, bi = bb_re[p:p + 1, :], bb_im[p:p + 1, :]
        w_ref[0, p * SSM_C:(p + 1) * SSM_C, :] = jnp.where(
            first, br * e_re - bi * e_im, br * e_im + bi * e_re).astype(BF16)

    lane_n = lax.broadcasted_iota(jnp.int32, (SSM_N, LANES), 1)
    tau = (lane_n % SSM_C).astype(F32)
    p0_re, p0_im = _cpow(ar_c, ai_c, dt, tau)
    ab_re_c, ab_im_c = p0_re[:, 1:2], p0_im[:, 1:2]
    p1_re = p0_re * ab_re_c - p0_im * ab_im_c
    p1_im = p0_re * ab_im_c + p0_im * ab_re_c
    reps = SSM_K // LANES
    p0_re, p0_im = jnp.concatenate([p0_re] * reps, 1), jnp.concatenate([p0_im] * reps, 1)
    p1_re, p1_im = jnp.concatenate([p1_re] * reps, 1), jnp.concatenate([p1_im] * reps, 1)

    sel = (lax.broadcasted_iota(jnp.int32, (SSM_P, SSM_K), 1) // SSM_C
           == lax.broadcasted_iota(jnp.int32, (SSM_P, SSM_K), 0)).astype(BF16)

    def expand(ct):
        hi = ct.astype(BF16)
        rest = ct - hi.astype(F32)
        mid = rest.astype(BF16)
        lo = (rest - mid.astype(F32)).astype(BF16)
        return _dot(hi, sel) + _dot(mid, sel) + _dot(lo, sel)

    c_re, c_im = expand(ct_ref[0, 0]), expand(ct_ref[0, 1])

    v_ref[0, 0:SSM_N, :] = (c_re * p1_re - c_im * p1_im).astype(BF16)
    v_ref[0, SSM_N:2 * SSM_N, :] = (-(c_re * p1_im + c_im * p1_re)).astype(BF16)

    ce = jnp.concatenate([c_re * p0_re - c_im * p0_im, -(c_re * p0_im + c_im * p0_re)], axis=0)
    kt = jnp.dot(jnp.where(first, bb_re, bb_im), ce, precision=lax.Precision.HIGHEST,
                 preferred_element_type=F32)

    row = lax.broadcasted_iota(jnp.int32, (SSM_C, SSM_K), 0)
    lane_t = lax.broadcasted_iota(jnp.int32, (SSM_C, SSM_K), 1) % SSM_C
    causal = lane_t >= row
    for p in range(SSM_P):
        base = jnp.broadcast_to(kt[p:p + 1, :], (SSM_C, SSM_K))
        shifted = pltpu.roll(base, 0, 1, stride=1, stride_axis=0)
        m_ref[0, p * SSM_C:(p + 1) * SSM_C, :] = jnp.where(causal, shifted, 0.0).astype(BF16)


def _ssm_prep(log_dt, a_row, a_col, b_t, c_t):
    g = log_dt.shape[0]
    return pl.pallas_call(
        _ssm_prep_kernel,
        grid=(g,),
        in_specs=[
            pl.BlockSpec((1, 1, 1), lambda i: (i, 0, 0)),
            pl.BlockSpec((1, 2, 2 * SSM_N), lambda i: (i, 0, 0)),
            pl.BlockSpec((1, SSM_N, 2), lambda i: (i, 0, 0)),
            pl.BlockSpec((1, 2, SSM_P, 2 * SSM_N), lambda i: (i, 0, 0, 0)),
            pl.BlockSpec((1, 2, SSM_N, SSM_P), lambda i: (i, 0, 0, 0)),
        ],
        out_specs=[
            pl.BlockSpec((1, SSM_K, SSM_K), lambda i: (i, 0, 0)),
            pl.BlockSpec((1, SSM_K, 2 * SSM_N), lambda i: (i, 0, 0)),
            pl.BlockSpec((1, 2 * SSM_N, SSM_K), lambda i: (i, 0, 0)),
            pl.BlockSpec((1, 2, 2 * SSM_N), lambda i: (i, 0, 0)),
        ],
        out_shape=[
            jax.ShapeDtypeStruct((g, SSM_K, SSM_K), BF16),
            jax.ShapeDtypeStruct((g, SSM_K, 2 * SSM_N), BF16),
            jax.ShapeDtypeStruct((g, 2 * SSM_N, SSM_K), BF16),
            jax.ShapeDtypeStruct((g, 2, 2 * SSM_N), F32),
        ],
        compiler_params=_cparams(1),
        name="ssm_prep",
    )(log_dt, a_row, a_col, b_t, c_t)


SUBLANES = 8


def _transpose8(tiles):
    sub = lax.broadcasted_iota(jnp.int32, tiles[0].shape, 0)
    t = list(tiles)
    for d in (4, 2, 1):
        clear = (sub & d) == 0
        for i0 in range(SUBLANES):
            if i0 & d:
                continue
            lo, hi = t[i0], t[i0 + d]
            t[i0] = jnp.where(clear, lo, pltpu.roll(hi, d, 0))
            t[i0 + d] = jnp.where(clear, pltpu.roll(lo, SUBLANES - d, 0), hi)
    return t


def _rows_to_lane_blocks(x):
    n_rows, width = x.shape
    n = width // LANES
    out_rows = []
    for k in range(n // SUBLANES):
        pieces = []
        for b in range(n_rows // SUBLANES):
            xb = x[b * SUBLANES:(b + 1) * SUBLANES]
            pieces += _transpose8([xb[:, (SUBLANES * k + i) * LANES:(SUBLANES * k + i + 1) * LANES]
                                   for i in range(SUBLANES)])
        out_rows.append(jnp.concatenate(pieces, axis=1))
    return jnp.concatenate(out_rows, axis=0)


def _lane_blocks_to_rows(y, n_rows):
    n = y.shape[0]
    out_rows = []
    for b in range(n_rows // SUBLANES):
        pieces = []
        for k in range(n // SUBLANES):
            yk = y[k * SUBLANES:(k + 1) * SUBLANES]
            pieces += _transpose8([yk[:, (SUBLANES * b + j) * LANES:(SUBLANES * b + j + 1) * LANES]
                                   for j in range(SUBLANES)])
        out_rows.append(jnp.concatenate(pieces, axis=1))
    return jnp.concatenate(out_rows, axis=0)

def _ssm_scan_kernel(u_ref, m_ref, w_ref, v_ref, ac_ref, d_ref, y_ref, z_scr, zs_scr, sp_scr,
                     *, bsz, n_chunks):
    pairs = n_chunks // 2
    n_pairs = bsz * pairs
    ut = u_ref[...]
    u2 = _rows_to_lane_blocks(ut)
    low = lax.broadcasted_iota(jnp.int32, (n_pairs, LANES), 1) < SSM_C

    def col(a, j):
        return a[:, j * LANES:(j + 1) * LANES]

    def swap(a):
        return pltpu.roll(a, SSM_C, 1)

    even = [jnp.where(low, col(u2, 2 * k), swap(col(u2, 2 * k + 1))) for k in range(SSM_P // 2)]
    odd = [jnp.where(low, swap(col(u2, 2 * k)), col(u2, 2 * k + 1)) for k in range(SSM_P // 2)]
    ub = jnp.concatenate([jnp.concatenate(even, axis=1), jnp.concatenate(odd, axis=1)],
                         axis=0).astype(BF16)
    y = _dot(ub, m_ref[0])
    z = _dot(ub, w_ref[0])
    z_scr[...] = z
    zs_scr[...] = pltpu.roll(z, SSM_N, 1)

    a1 = ac_ref[0, 0:1, :]
    a2 = ac_ref[0, 1:2, :]
    s = jnp.zeros((bsz, 2 * SSM_N), F32)
    ss = jnp.zeros((bsz, 2 * SSM_N), F32)
    for c in range(n_chunks):
        rows = pl.ds((c % 2) * n_pairs + c // 2, bsz, stride=pairs)
        sp_scr[rows, :] = s
        if c + 1 < n_chunks:
            s, ss = (a1 * s + a2 * ss + z_scr[rows, :],
                     a1 * ss - a2 * s + zs_scr[rows, :])
    y = y + _dot(sp_scr[...].astype(BF16), v_ref[0])
    y_even, y_odd = y[:n_pairs], y[n_pairs:]
    cols = []
    for k in range(SSM_P // 2):
        ce, co = col(y_even, k), col(y_odd, k)
        cols += [jnp.where(low, ce, swap(co)), jnp.where(low, swap(ce), co)]
    y2 = jnp.concatenate(cols, axis=1)
    yt = _lane_blocks_to_rows(y2, SSM_P)
    y_ref[...] = yt + d_ref[...] * ut


def _ssm_scan(zt, m, w, v, ac, d_col, *, bsz):
    n_tok = zt.shape[1]
    n_rows = n_tok // SSM_C
    n_chunks = n_rows // bsz
    kern = functools.partial(_ssm_scan_kernel, bsz=bsz, n_chunks=n_chunks)
    return pl.pallas_call(
        kern,
        grid=(SSM_G,),
        in_specs=[
            pl.BlockSpec((SSM_P, n_tok), lambda i: (i, 0)),
            pl.BlockSpec((1, SSM_K, SSM_K), lambda i: (i, 0, 0)),
            pl.BlockSpec((1, SSM_K, 2 * SSM_N), lambda i: (i, 0, 0)),
            pl.BlockSpec((1, 2 * SSM_N, SSM_K), lambda i: (i, 0, 0)),
            pl.BlockSpec((1, 2, 2 * SSM_N), lambda i: (i, 0, 0)),
            pl.BlockSpec((SSM_P, 1), lambda i: (i, 0)),
        ],
        out_specs=pl.BlockSpec((SSM_P, n_tok), lambda i: (i, 0)),
        out_shape=jax.ShapeDtypeStruct((TOK_WIDTH, n_tok), F32),
        scratch_shapes=[pltpu.VMEM((n_rows, 2 * SSM_N), F32)] * 3,
        compiler_params=_cparams(1),
        name="ssm_scan",
    )(zt, m, w, v, ac, d_col)


ATTN_BLOCK = 256
KV_SUB = 2


def _shared_kv_kernel(x_ref, g_ref, wk_ref, wv_ref, kg_ref, k_ref, vt_ref):
    h = _rms(x_ref[...], g_ref[...]).astype(BF16)
    k_ref[...] = _seg_rms64(_dot(h, wk_ref[...].astype(BF16)), kg_ref[...]).astype(BF16)
    v = _dot(h, wv_ref[...].astype(BF16))
    for j in range(vt_ref.shape[0]):
        vt_ref[j] = v[j * ATTN_BLOCK:(j + 1) * ATTN_BLOCK].T.astype(BF16)


def _shared_kv(x, g, wk, wv, k_gain, tm=512):
    n_tok = x.shape[0]
    sub = tm // ATTN_BLOCK
    return pl.pallas_call(
        _shared_kv_kernel,
        grid=(n_tok // tm,),
        in_specs=[
            pl.BlockSpec((tm, D_MODEL), lambda i: (i, 0)),
            _const_spec((1, D_MODEL)),
            _const_spec(wk.shape),
            _const_spec(wv.shape),
            _const_spec((1, TOK_WIDTH)),
        ],
        out_specs=[pl.BlockSpec((tm, TOK_WIDTH), lambda i: (i, 0)),
                   pl.BlockSpec((sub, TOK_WIDTH, ATTN_BLOCK), lambda i: (i, 0, 0))],
        out_shape=[jax.ShapeDtypeStruct((n_tok, TOK_WIDTH), BF16),
                   jax.ShapeDtypeStruct((n_tok // ATTN_BLOCK, TOK_WIDTH, ATTN_BLOCK), BF16)],
        compiler_params=_cparams(1),
        name="shared_kv",
    )(x, g, wk, wv, k_gain)


N_STREAMS = 2 * DIFF_HEADS


def _diff_attn_kernel(qt_ref, k_ref, vt_ref, lam_ref, subg_ref, o_ref, qm_scr, sc_scr,
                      bm_scr, m_scr, acc_scr, *, lambda_init):
    tq = ATTN_BLOCK
    qi = pl.program_id(1)
    lp = lam_ref[...]
    lam = (jnp.exp(jnp.sum(lp[0:1] * lp[1:2], axis=-1, keepdims=True))
           - jnp.exp(jnp.sum(lp[2:3] * lp[3:4], axis=-1, keepdims=True)) + lambda_init)
    key_chunk = lax.broadcasted_iota(jnp.int32, (tq, tq), 0) // CHUNK
    qry_chunk = lax.broadcasted_iota(jnp.int32, (tq, tq), 1) // CHUNK
    diag_mask = key_chunk <= qry_chunk
    first_half = lax.broadcasted_iota(jnp.int32, (LANES, tq), 0) < DIFF_HEAD_DIM
    n_grp = DIFF_HEADS // 2

    for kg in range(2 * n_grp):
        qg = qt_ref[kg * LANES:(kg + 1) * LANES, :]
        zero = jnp.zeros_like(qg)
        qm_scr[kg, :, 0:tq] = jnp.where(first_half, qg, zero)
        qm_scr[kg, :, tq:2 * tq] = jnp.where(first_half, zero, qg)

    def block(j, diagonal, n_sub=1):
        nk = n_sub * tq
        rows = pl.ds(pl.multiple_of(j * tq, tq), nk)
        for kg in range(2 * n_grp):
            kb = k_ref[rows, kg * LANES:(kg + 1) * LANES]
            st = _dot(kb, qm_scr[kg])
            if diagonal:
                st = jnp.where(jnp.concatenate([diag_mask, diag_mask], axis=1), st, -jnp.inf)
            sc_scr[kg, 0:nk, :] = st
            bm_scr[kg] = jnp.max(st, axis=0, keepdims=True)
        for kg in range(2 * n_grp):
            g = kg % n_grp
            if diagonal:
                m_new = bm_scr[kg]
            else:
                m_old = m_scr[kg]
                m_new = jnp.maximum(m_old, bm_scr[kg])
                alpha = jnp.exp2(m_old - m_new)
            pb = jnp.exp2(sc_scr[kg, 0:nk, :] - m_new).astype(BF16)
            m_scr[kg] = m_new

            def v_rows(h):
                vt = [vt_ref[j + i, h * DIFF_V_DIM:(h + 1) * DIFF_V_DIM, :] for i in range(n_sub)]
                return _with_ones_rows(vt[0] if n_sub == 1 else jnp.concatenate(vt, axis=1))

            pv = jnp.concatenate(
                [_dot(v_rows(h), pb[:, hh * tq:(hh + 1) * tq])
                 for hh, h in enumerate((2 * g, 2 * g + 1))], axis=1)
            acc_scr[kg] = pv if diagonal else alpha * acc_scr[kg] + pv

    def body(t, carry):
        block(KV_SUB * t, False, KV_SUB)
        return carry

    block(qi, True)
    lax.fori_loop(0, qi // KV_SUB, body, 0)
    for r in range(1, KV_SUB):
        @pl.when(qi % KV_SUB >= r)
        def _():
            block(qi - qi % KV_SUB + r - 1, False)

    for head in range(DIFF_HEADS):
        rows = slice(head * DIFF_V_DIM, (head + 1) * DIFF_V_DIM)
        q_cols = slice((head % 2) * tq, (head % 2 + 1) * tq)
        on = []
        for comp in range(2):
            acc = acc_scr[comp * n_grp + head // 2]
            on.append(acc[0:DIFF_V_DIM, q_cols] * (1.0 / acc[DIFF_V_DIM:DIFF_V_DIM + 1, q_cols]))
        o = on[0] - lam * on[1]
        o = o * lax.rsqrt(jnp.mean(o * o, axis=0, keepdims=True) + EPS)
        o_ref[rows, :] = (o * subg_ref[rows, :] * (1.0 - lambda_init)).astype(o_ref.dtype)


def _diff_attn(qt, k, vt, lam_params, sub_gain, *, bsz, seq, lambda_init):
    tq = ATTN_BLOCK
    n_q = seq // tq
    return pl.pallas_call(
        functools.partial(_diff_attn_kernel, lambda_init=lambda_init),
        grid=(bsz, n_q),
        in_specs=[
            pl.BlockSpec((TOK_WIDTH, tq), lambda b, i: (0, b * n_q + i)),
            pl.BlockSpec((seq, TOK_WIDTH), lambda b, i: (b, 0)),
            pl.BlockSpec((n_q, TOK_WIDTH, tq), lambda b, i: (b, 0, 0)),
            pl.BlockSpec((4, DIFF_HEAD_DIM), lambda b, i: (0, 0)),
            pl.BlockSpec((TOK_WIDTH, 1), lambda b, i: (0, 0)),
        ],
        out_specs=pl.BlockSpec((TOK_WIDTH, tq), lambda b, i: (0, b * n_q + i)),
        out_shape=jax.ShapeDtypeStruct((TOK_WIDTH, bsz * seq), F32),
        scratch_shapes=[pltpu.VMEM((DIFF_HEADS, LANES, 2 * tq), BF16),
                        pltpu.VMEM((DIFF_HEADS, KV_SUB * tq, 2 * tq), F32),
                        pltpu.VMEM((DIFF_HEADS, 1, 2 * tq), F32),
                        pltpu.VMEM((DIFF_HEADS, 1, 2 * tq), F32),
                        pltpu.VMEM((DIFF_HEADS, DIFF_V_DIM + ONES_ROWS, 2 * tq), F32)],
        compiler_params=_cparams(2),
        name="diff_attn",
    )(qt, k, vt, lam_params, sub_gain)


def _gelu_tanh(x):
    return 0.5 * x * (1.0 + jnp.tanh(math.sqrt(2.0 / math.pi) * (x + 0.044715 * (x * x * x))))


def _sigmoid(x):
    return 1.0 / (1.0 + jnp.exp(-x))


POST_SUB_TILES = 2


def _post_kernel(tok_ref, omem_ref, x_ref, wglu_ref, wout_ref, g_ref, wg_ref, wu_ref,
                 wd_ref, o_ref, *, s5):
    sub = x_ref.shape[0] // POST_SUB_TILES
    fronts = []
    for i in range(POST_SUB_TILES):
        cols = slice(i * sub, (i + 1) * sub)
        tok = tok_ref[:, cols].T
        if s5:
            h = _gelu_tanh(tok)
            gate = _sigmoid(_dot(h.astype(BF16), wglu_ref[0].astype(BF16)))
            tok = h * gate
        mix = jnp.concatenate([tok.astype(BF16), omem_ref[:, cols].T.astype(BF16)], axis=1)
        x1 = x_ref[cols, :] + _dot(mix, wout_ref[0].astype(BF16))
        fronts.append((x1, _rms(x1, g_ref[...]).astype(BF16)))
    for i, (x1, h2) in enumerate(fronts):
        gate = _dot(h2, wg_ref[0])
        up = _dot(h2, wu_ref[0])
        act = (gate * _sigmoid(gate) * up).astype(BF16)
        o_ref[i * sub:(i + 1) * sub, :] = x1 + _dot(act, wd_ref[0])


def _post(tok, o_mem, x, w_glu, w_out, g, wg, wu, wd, *, layer, glu_layer, s5, tm=512):
    n_tok = x.shape[0]
    return pl.pallas_call(
        functools.partial(_post_kernel, s5=s5),
        grid=(n_tok // tm,),
        in_specs=[
            pl.BlockSpec((TOK_WIDTH, tm), lambda i: (0, i)),
            pl.BlockSpec((MEM_WIDTH, tm), lambda i: (0, i)),
            pl.BlockSpec((tm, D_MODEL), lambda i: (i, 0)),
            _layer_spec(w_glu.shape, glu_layer),
            _layer_spec(w_out.shape, layer),
            _const_spec((1, D_MODEL)),
            _layer_spec(wg.shape, layer),
            _layer_spec(wu.shape, layer),
            _layer_spec(wd.shape, layer),
        ],
        out_specs=pl.BlockSpec((tm, D_MODEL), lambda i: (i, 0)),
        out_shape=jax.ShapeDtypeStruct((n_tok, D_MODEL), F32),
        compiler_params=_cparams(1),
        name="post_s5" if s5 else "post",
    )(tok, o_mem, x, w_glu, w_out, g, wg, wu, wd)


def _tile_gain(g, reps):
    return jnp.tile(g.astype(F32), reps)[None, :]


def kernel(x, mem, norm_mix, norm_ffn, norm_mem, w_in, w_out, mem_wk, mem_wv, mem_q_norm,
           mem_k_norm, ffn_w_gate, ffn_w_up, ffn_w_down, ssm_a_re, ssm_a_im, ssm_log_dt,
           ssm_b_re, ssm_b_im, ssm_c_re, ssm_c_im, ssm_d, ssm_w_glu, kv_norm, diff_wk, diff_wv,
           diff_k_norm, diff_q_norm, diff_lambda_q1, diff_lambda_k1, diff_lambda_q2,
           diff_lambda_k2, diff_sub_norm):
    bsz, seq, _ = x.shape
    n_tok = bsz * seq
    xf = x.reshape(n_tok, D_MODEL)

    k_gain = jnp.tile(mem_k_norm.astype(F32), (1, MEM_HEADS))[:, None, :]
    k_mem, vt_mem = _mem_kv(mem, norm_mem[:, None, :], mem_wk, mem_wv, k_gain)

    ffn_gate, ffn_up, ffn_down = (w.astype(BF16) for w in (ffn_w_gate, ffn_w_up, ffn_w_down))

    k_sh = v_sh = None
    for layer in range(DEPTH):
        s5 = layer < N_A_LAYERS
        q_gain = _tile_gain(mem_q_norm[layer], MEM_HEADS).T
        if s5:
            tok_gain = jnp.ones((TOK_WIDTH, 1), F32)
        else:
            tok_gain = _tile_gain(diff_q_norm[layer - N_A_LAYERS], 2 * DIFF_HEADS).T
        z_tok, o_mem = _in_proj(xf, norm_mix[layer][None, :], w_in,
                                tok_gain, q_gain, k_mem, vt_mem,
                                layer=layer, seq=seq, qk_norm=not s5)
        if s5:
            i = layer
            log_dt = ssm_log_dt[i][:, None, None]
            a_row = jnp.tile(jnp.stack([ssm_a_re[i], ssm_a_im[i]], axis=1), (1, 1, 2))
            a_col = jnp.stack([ssm_a_re[i], ssm_a_im[i]], axis=2)
            b_t = jnp.tile(jnp.stack([ssm_b_re[i], ssm_b_im[i]], axis=1).transpose(0, 1, 3, 2),
                           (1, 1, 1, 2))
            c_t = jnp.stack([ssm_c_re[i], ssm_c_im[i]], axis=1).transpose(0, 1, 3, 2)
            m_op, w_op, v_op, ac = _ssm_prep(log_dt, a_row, a_col, b_t, c_t)
            tok = _ssm_scan(z_tok, m_op, w_op, v_op, ac, ssm_d[i][:, None], bsz=bsz)
        else:
            j = layer - N_A_LAYERS
            lambda_init = 0.8 - 0.6 * math.exp(-0.3 * layer)
            lam_params = jnp.stack([diff_lambda_q1[j], diff_lambda_k1[j],
                                    diff_lambda_q2[j], diff_lambda_k2[j]]).astype(F32)
            tok = _diff_attn(z_tok, k_sh, v_sh, lam_params,
                             _tile_gain(diff_sub_norm[j], DIFF_HEADS).T,
                             bsz=bsz, seq=seq, lambda_init=lambda_init)
        xf = _post(tok, o_mem, xf, ssm_w_glu, w_out, norm_ffn[layer][None, :],
                   ffn_gate, ffn_up, ffn_down,
                   layer=layer, glu_layer=layer if s5 else 0, s5=s5)
        if layer == N_A_LAYERS - 1:
            k_sh, v_sh = _shared_kv(xf, kv_norm[None, :], diff_wk, diff_wv,
                                    _tile_gain(diff_k_norm, 2 * DIFF_HEADS))
    return xf.reshape(bsz, seq, D_MODEL)
```

```python
import functools
import math

import jax
import jax.numpy as jnp
from jax import lax
from jax.experimental import pallas as pl
from jax.experimental.pallas import tpu as pltpu

D_MODEL = 1024
DEPTH = 4
CHUNK = 64
N_A_LAYERS = DEPTH // 2
MEM_HEADS = 4
MEM_HEAD_DIM = 64
MEM_WIDTH = MEM_HEADS * MEM_HEAD_DIM
TOK_WIDTH = D_MODEL - MEM_WIDTH
SSM_P = 16
SSM_G = TOK_WIDTH // SSM_P
SSM_N = 64
SSM_C = 64
SSM_K = SSM_P * SSM_C
SCAN_GROUPS = 4
DIFF_HEAD_DIM = 64
DIFF_HEADS = TOK_WIDTH // (2 * DIFF_HEAD_DIM)
DIFF_V_DIM = 2 * DIFF_HEAD_DIM
D_FF = 2816
EPS = 1e-6

LANES = 128
VMEM_LIMIT = 56 * 1024 * 1024

F32 = jnp.float32
BF16 = jnp.bfloat16


def _cparams(n_grid, vmem=VMEM_LIMIT, flags=None):
    return pltpu.CompilerParams(
        dimension_semantics=("arbitrary",) * n_grid, vmem_limit_bytes=vmem, flags=flags)


def _const_spec(shape):
    nd = len(shape)
    return pl.BlockSpec(shape, lambda *_: (0,) * nd, pipeline_mode=pl.Buffered(1))


def _layer_spec(shape, layer):
    nd = len(shape)
    return pl.BlockSpec((1,) + tuple(shape[1:]), lambda *_: (layer,) + (0,) * (nd - 1),
                        pipeline_mode=pl.Buffered(1))


def _rms(x, g):
    return x * lax.rsqrt(jnp.mean(x * x, axis=-1, keepdims=True) + EPS) * g


def _seg_rms64(x, g):
    rows, width = x.shape
    lane = lax.broadcasted_iota(jnp.int32, (rows, LANES), 1)
    lo = lane < 64
    outs = []
    for v in range(width // LANES):
        xv = x[:, v * LANES:(v + 1) * LANES]
        sq = xv * xv
        s_lo = jnp.sum(jnp.where(lo, sq, 0.0), axis=-1, keepdims=True)
        s_hi = jnp.sum(jnp.where(lo, 0.0, sq), axis=-1, keepdims=True)
        r_lo = lax.rsqrt(s_lo * (1.0 / 64) + EPS)
        r_hi = lax.rsqrt(s_hi * (1.0 / 64) + EPS)
        outs.append(xv * jnp.where(lo, r_lo, r_hi) * g[:, v * LANES:(v + 1) * LANES])
    return jnp.concatenate(outs, axis=1)


def _half_masks(rows, dtype):
    lane = lax.broadcasted_iota(jnp.int32, (rows, LANES), 1)
    lo = (lane < 64).astype(dtype)
    return lo, (1 - lo).astype(dtype)


def _dot(a, b):
    return jnp.dot(a, b, preferred_element_type=F32)


def _dot_nt(a, b):
    return lax.dot_general(a, b, (((1,), (1,)), ((), ())), preferred_element_type=F32)


def _mem_kv_kernel(mem_ref, g_ref, wk_ref, wv_ref, kg_ref, k_ref, vt_ref):
    bsz, n_mem, _ = mem_ref.shape
    mn = _rms(mem_ref[...].reshape(bsz * n_mem, D_MODEL), g_ref[0]).astype(BF16)
    k = _seg_rms64(_dot(mn, wk_ref[0].astype(BF16)), kg_ref[0])
    k_ref[0] = k.reshape(bsz, n_mem, MEM_WIDTH).astype(BF16)
    v = _dot(mn, wv_ref[0].astype(BF16))
    for b in range(bsz):
        vt_ref[0, b] = v[b * n_mem:(b + 1) * n_mem].T.astype(BF16)


def _mem_kv(mem, norm_mem, wk, wv, k_gain):
    bsz, n_mem, _ = mem.shape
    return pl.pallas_call(
        _mem_kv_kernel,
        grid=(DEPTH,),
        in_specs=[
            _const_spec(mem.shape),
            pl.BlockSpec((1, 1, D_MODEL), lambda l: (l, 0, 0)),
            pl.BlockSpec((1, D_MODEL, MEM_WIDTH), lambda l: (l, 0, 0)),
            pl.BlockSpec((1, D_MODEL, MEM_WIDTH), lambda l: (l, 0, 0)),
            pl.BlockSpec((1, 1, MEM_WIDTH), lambda l: (l, 0, 0)),
        ],
        out_specs=[pl.BlockSpec((1, bsz, n_mem, MEM_WIDTH), lambda l: (l, 0, 0, 0)),
                   pl.BlockSpec((1, bsz, MEM_WIDTH, n_mem), lambda l: (l, 0, 0, 0))],
        out_shape=[jax.ShapeDtypeStruct((DEPTH, bsz, n_mem, MEM_WIDTH), BF16),
                   jax.ShapeDtypeStruct((DEPTH, bsz, MEM_WIDTH, n_mem), BF16)],
        compiler_params=_cparams(1),
        name="mem_kv",
    )(mem, norm_mem, wk, wv, k_gain)


def _seg_rms_rows(xt, g_col, seg):
    width, cols = xt.shape
    x3 = xt.reshape(width // seg, seg, cols)
    ms = jnp.mean(x3 * x3, axis=1, keepdims=True)
    return (x3 * lax.rsqrt(ms + EPS)).reshape(width, cols) * g_col


ONES_ROWS = 16
SOFTMAX_LOG2E = math.log2(math.e)


def _with_ones_rows(vt):
    return jnp.concatenate([vt, jnp.ones((ONES_ROWS, vt.shape[1]), vt.dtype)], axis=0)


def _mem_attention_t(qt, k, vt):
    tm = qt.shape[1]
    first = lax.broadcasted_iota(jnp.int32, (LANES, tm), 0) < MEM_HEAD_DIM
    scores = []
    for grp in range(MEM_WIDTH // LANES):
        rows = slice(grp * LANES, (grp + 1) * LANES)
        qg = qt[rows, :]
        zero = jnp.zeros_like(qg)
        qm = jnp.concatenate([jnp.where(first, qg, zero), jnp.where(first, zero, qg)], axis=1)
        scores.append(_dot(k[:, rows], qm))
    probs = [jnp.exp2(s - jnp.max(s, axis=0, keepdims=True)).astype(BF16)
             for s in scores]
    outs = []
    for head in range(MEM_HEADS):
        v_rows = _with_ones_rows(vt[head * MEM_HEAD_DIM:(head + 1) * MEM_HEAD_DIM, :])
        o = _dot(v_rows, probs[head // 2][:, (head % 2) * tm:(head % 2 + 1) * tm])
        outs.append(o[:MEM_HEAD_DIM] * (1.0 / o[MEM_HEAD_DIM:MEM_HEAD_DIM + 1]))
    return jnp.concatenate(outs, axis=0)


def _in_proj_kernel(x_ref, g_ref, win_ref, tokg_ref, qg_ref, k_ref, vt_ref,
                    ztok_ref, omem_ref, wt_scr, *, qk_norm):
    @pl.when(pl.program_id(0) == 0)
    def _():
        wt_scr[...] = win_ref[0].T.astype(BF16)

    h = _rms(x_ref[...], g_ref[...]).astype(BF16)
    zt = _dot_nt(wt_scr[...], h)
    z_tok = zt[:TOK_WIDTH]
    if qk_norm:
        z_tok = (_seg_rms_rows(z_tok, tokg_ref[...], DIFF_HEAD_DIM)
                 * (DIFF_HEAD_DIM ** -0.5 * SOFTMAX_LOG2E))
    ztok_ref[...] = z_tok.astype(ztok_ref.dtype)
    q_mem = (_seg_rms_rows(zt[TOK_WIDTH:], qg_ref[...], MEM_HEAD_DIM)
             * (MEM_HEAD_DIM ** -0.5 * SOFTMAX_LOG2E))
    omem_ref[...] = _mem_attention_t(q_mem.astype(BF16), k_ref[0, 0],
                                     vt_ref[0, 0]).astype(omem_ref.dtype)


def _in_proj(x, g, w_in, tok_gain, q_gain, k_mem, vt_mem, *, layer, seq, qk_norm, tm=1024):
    n_tok = x.shape[0]
    n_mem = k_mem.shape[2]
    tiles_per_seq = seq // tm
    return pl.pallas_call(
        functools.partial(_in_proj_kernel, qk_norm=qk_norm),
        grid=(n_tok // tm,),
        in_specs=[
            pl.BlockSpec((tm, D_MODEL), lambda i: (i, 0)),
            _const_spec((1, D_MODEL)),
            _layer_spec(w_in.shape, layer),
            _const_spec((TOK_WIDTH, 1)),
            _const_spec((MEM_WIDTH, 1)),
            pl.BlockSpec((1, 1, n_mem, MEM_WIDTH), lambda i: (layer, i // tiles_per_seq, 0, 0)),
            pl.BlockSpec((1, 1, MEM_WIDTH, n_mem), lambda i: (layer, i // tiles_per_seq, 0, 0)),
        ],
        out_specs=[pl.BlockSpec((TOK_WIDTH, tm), lambda i: (0, i)),
                   pl.BlockSpec((MEM_WIDTH, tm), lambda i: (0, i))],
        out_shape=[jax.ShapeDtypeStruct((TOK_WIDTH, n_tok), BF16 if qk_norm else F32),
                   jax.ShapeDtypeStruct((MEM_WIDTH, n_tok), F32)],
        scratch_shapes=[pltpu.VMEM((D_MODEL, D_MODEL), BF16)],
        compiler_params=_cparams(1),
        name="in_proj_q" if qk_norm else "in_proj_u",
    )(x, g, w_in, tok_gain, q_gain, k_mem, vt_mem)


def _cpow(ar, ai, dt, k):
    mag = jnp.exp(k * dt * ar)
    ang = k * dt * ai
    return mag * jnp.cos(ang), mag * jnp.sin(ang)


def _zoh_coef(ar, ai, abr, abi):
    den = ar * ar + ai * ai
    nr = abr - 1.0
    return (nr * ar + abi * ai) / den, (abi * ar - nr * ai) / den


def _ssm_prep_kernel(*refs):
    for gi in range(SCAN_GROUPS):
        _ssm_prep_group(*(r.at[pl.ds(gi, 1)] for r in refs))


def _ssm_prep_group(ldt_ref, arow_ref, acol_ref, bt_ref, ct_ref, m_ref, w_ref, v_ref, ac_ref):
    dt = jnp.exp(ldt_ref[0])
    first = lax.broadcasted_iota(jnp.int32, (1, LANES), 1) < SSM_N
    ar_r, ai_r = arow_ref[0, 0:1, :], arow_ref[0, 1:2, :]
    ar_c, ai_c = acol_ref[0, :, 0:1], acol_ref[0, :, 1:2]

    s_row = lax.broadcasted_iota(jnp.int32, (SSM_C, LANES), 0).astype(F32)
    e_re, e_im = _cpow(ar_r, ai_r, dt, float(SSM_C - 1) - s_row)
    ab_re, ab_im = e_re[SSM_C - 2:SSM_C - 1], e_im[SSM_C - 2:SSM_C - 1]
    cf_re, cf_im = _zoh_coef(ar_r, ai_r, ab_re, ab_im)

    pc_re = e_re[0:1] * ab_re - e_im[0:1] * ab_im
    pc_im = e_re[0:1] * ab_im + e_im[0:1] * ab_re
    ac_ref[0, 0:1, :] = pc_re
    ac_ref[0, 1:2, :] = jnp.where(first, -pc_im, pc_im)

    bre, bim = bt_ref[0, 0], bt_ref[0, 1]
    bb_re = cf_re * bre - cf_im * bim
    bb_im = cf_re * bim + cf_im * bre

    for p in range(SSM_P):
        br, bi = bb_re[p:p + 1, :], bb_im[p:p + 1, :]
        w_ref[0, p * SSM_C:(p + 1) * SSM_C, :] = jnp.where(
            first, br * e_re - bi * e_im, br * e_im + bi * e_re).astype(BF16)

    lane_n = lax.broadcasted_iota(jnp.int32, (SSM_N, LANES), 1)
    tau = (lane_n % SSM_C).astype(F32)
    p0_re, p0_im = _cpow(ar_c, ai_c, dt, tau)
    ab_re_c, ab_im_c = p0_re[:, 1:2], p0_im[:, 1:2]
    p1_re = p0_re * ab_re_c - p0_im * ab_im_c
    p1_im = p0_re * ab_im_c + p0_im * ab_re_c
    reps = SSM_K // LANES
    p0_re, p0_im = jnp.concatenate([p0_re] * reps, 1), jnp.concatenate([p0_im] * reps, 1)
    p1_re, p1_im = jnp.concatenate([p1_re] * reps, 1), jnp.concatenate([p1_im] * reps, 1)

    sel = (lax.broadcasted_iota(jnp.int32, (SSM_P, SSM_K), 1) // SSM_C
           == lax.broadcasted_iota(jnp.int32, (SSM_P, SSM_K), 0)).astype(BF16)

    def expand(ct):
        hi = ct.astype(BF16)
        rest = ct - hi.astype(F32)
        mid = rest.astype(BF16)
        lo = (rest - mid.astype(F32)).astype(BF16)
        return _dot(hi, sel) + _dot(mid, sel) + _dot(lo, sel)

    c_re, c_im = expand(ct_ref[0, 0]), expand(ct_ref[0, 1])

    v_ref[0, 0:SSM_N, :] = (c_re * p1_re - c_im * p1_im).astype(BF16)
    v_ref[0, SSM_N:2 * SSM_N, :] = (-(c_re * p1_im + c_im * p1_re)).astype(BF16)

    ce = jnp.concatenate([c_re * p0_re - c_im * p0_im, -(c_re * p0_im + c_im * p0_re)], axis=0)
    kt = jnp.dot(jnp.where(first, bb_re, bb_im), ce, precision=lax.Precision.HIGHEST,
                 preferred_element_type=F32)

    row = lax.broadcasted_iota(jnp.int32, (SSM_C, SSM_K), 0)
    lane_t = lax.broadcasted_iota(jnp.int32, (SSM_C, SSM_K), 1) % SSM_C
    causal = lane_t >= row
    for p in range(SSM_P):
        base = jnp.broadcast_to(kt[p:p + 1, :], (SSM_C, SSM_K))
        shifted = pltpu.roll(base, 0, 1, stride=1, stride_axis=0)
        m_ref[0, p * SSM_C:(p + 1) * SSM_C, :] = jnp.where(causal, shifted, 0.0).astype(BF16)


def _ssm_prep(log_dt, a_row, a_col, b_t, c_t):
    g = log_dt.shape[0]
    return pl.pallas_call(
        _ssm_prep_kernel,
        grid=(g // SCAN_GROUPS,),
        in_specs=[
            pl.BlockSpec((SCAN_GROUPS, 1, 1), lambda i: (i, 0, 0)),
            pl.BlockSpec((SCAN_GROUPS, 2, 2 * SSM_N), lambda i: (i, 0, 0)),
            pl.BlockSpec((SCAN_GROUPS, SSM_N, 2), lambda i: (i, 0, 0)),
            pl.BlockSpec((SCAN_GROUPS, 2, SSM_P, 2 * SSM_N), lambda i: (i, 0, 0, 0)),
            pl.BlockSpec((SCAN_GROUPS, 2, SSM_N, SSM_P), lambda i: (i, 0, 0, 0)),
        ],
        out_specs=[
            pl.BlockSpec((SCAN_GROUPS, SSM_K, SSM_K), lambda i: (i, 0, 0)),
            pl.BlockSpec((SCAN_GROUPS, SSM_K, 2 * SSM_N), lambda i: (i, 0, 0)),
            pl.BlockSpec((SCAN_GROUPS, 2 * SSM_N, SSM_K), lambda i: (i, 0, 0)),
            pl.BlockSpec((SCAN_GROUPS, 2, 2 * SSM_N), lambda i: (i, 0, 0)),
        ],
        out_shape=[
            jax.ShapeDtypeStruct((g, SSM_K, SSM_K), BF16),
            jax.ShapeDtypeStruct((g, SSM_K, 2 * SSM_N), BF16),
            jax.ShapeDtypeStruct((g, 2 * SSM_N, SSM_K), BF16),
            jax.ShapeDtypeStruct((g, 2, 2 * SSM_N), F32),
        ],
        compiler_params=_cparams(1),
        name="ssm_prep",
    )(log_dt, a_row, a_col, b_t, c_t)


SUBLANES = 8


def _transpose8(tiles):
    sub = lax.broadcasted_iota(jnp.int32, tiles[0].shape, 0)
    t = list(tiles)
    for d in (4, 2, 1):
        clear = (sub & d) == 0
        for i0 in range(SUBLANES):
            if i0 & d:
                continue
            lo, hi = t[i0], t[i0 + d]
            t[i0] = jnp.where(clear, lo, pltpu.roll(hi, d, 0))
            t[i0 + d] = jnp.where(clear, pltpu.roll(lo, SUBLANES - d, 0), hi)
    return t


def _rows_to_lane_blocks(x):
    n_rows, width = x.shape
    n = width // LANES
    out_rows = []
    for k in range(n // SUBLANES):
        pieces = []
        for b in range(n_rows // SUBLANES):
            xb = x[b * SUBLANES:(b + 1) * SUBLANES]
            pieces += _transpose8([xb[:, (SUBLANES * k + i) * LANES:(SUBLANES * k + i + 1) * LANES]
                                   for i in range(SUBLANES)])
        out_rows.append(jnp.concatenate(pieces, axis=1))
    return jnp.concatenate(out_rows, axis=0)


def _lane_blocks_to_rows(y, n_rows):
    n = y.shape[0]
    out_rows = []
    for b in range(n_rows // SUBLANES):
        pieces = []
        for k in range(n // SUBLANES):
            yk = y[k * SUBLANES:(k + 1) * SUBLANES]
            pieces += _transpose8([yk[:, (SUBLANES * b + j) * LANES:(SUBLANES * b + j + 1) * LANES]
                                   for j in range(SUBLANES)])
        out_rows.append(jnp.concatenate(pieces, axis=1))
    return jnp.concatenate(out_rows, axis=0)

def _ssm_scan_kernel(u_ref, m_ref, w_ref, v_ref, ac_ref, d_ref, y_ref, z_scr, zs_scr, sp_scr,
                     *, bsz, n_chunks):
    for gi in range(SCAN_GROUPS):
        ch = pl.ds(gi * SSM_P, SSM_P)
        one = pl.ds(gi, 1)
        _ssm_scan_group(u_ref.at[ch, :], m_ref.at[one], w_ref.at[one], v_ref.at[one],
                        ac_ref.at[one], d_ref.at[ch, :], y_ref.at[ch, :],
                        z_scr.at[gi], zs_scr.at[gi], sp_scr.at[gi], bsz=bsz, n_chunks=n_chunks)


def _ssm_scan_group(u_ref, m_ref, w_ref, v_ref, ac_ref, d_ref, y_ref, z_scr, zs_scr, sp_scr,
                    *, bsz, n_chunks):
    pairs = n_chunks // 2
    n_pairs = bsz * pairs
    ut = u_ref[...]
    u2 = _rows_to_lane_blocks(ut)
    low = lax.broadcasted_iota(jnp.int32, (n_pairs, LANES), 1) < SSM_C

    def col(a, j):
        return a[:, j * LANES:(j + 1) * LANES]

    def swap(a):
        return pltpu.roll(a, SSM_C, 1)

    even = [jnp.where(low, col(u2, 2 * k), swap(col(u2, 2 * k + 1))) for k in range(SSM_P // 2)]
    odd = [jnp.where(low, swap(col(u2, 2 * k)), col(u2, 2 * k + 1)) for k in range(SSM_P // 2)]
    ub = jnp.concatenate([jnp.concatenate(even, axis=1), jnp.concatenate(odd, axis=1)],
                         axis=0).astype(BF16)
    y = _dot(ub, m_ref[0])
    z = _dot(ub, w_ref[0])
    z_scr[...] = z
    zs_scr[...] = pltpu.roll(z, SSM_N, 1)

    a1 = ac_ref[0, 0:1, :]
    a2 = ac_ref[0, 1:2, :]
    s = jnp.zeros((bsz, 2 * SSM_N), F32)
    ss = jnp.zeros((bsz, 2 * SSM_N), F32)
    for c in range(n_chunks):
        rows = pl.ds((c % 2) * n_pairs + c // 2, bsz, stride=pairs)
        sp_scr[rows, :] = s
        if c + 1 < n_chunks:
            s, ss = (a1 * s + a2 * ss + z_scr[rows, :],
                     a1 * ss - a2 * s + zs_scr[rows, :])
    y = y + _dot(sp_scr[...].astype(BF16), v_ref[0])
    y_even, y_odd = y[:n_pairs], y[n_pairs:]
    cols = []
    for k in range(SSM_P // 2):
        ce, co = col(y_even, k), col(y_odd, k)
        cols += [jnp.where(low, ce, swap(co)), jnp.where(low, swap(ce), co)]
    y2 = jnp.concatenate(cols, axis=1)
    yt = _lane_blocks_to_rows(y2, SSM_P)
    y_ref[...] = yt + d_ref[...] * ut


def _ssm_scan(zt, m, w, v, ac, d_col, *, bsz):
    n_tok = zt.shape[1]
    n_rows = n_tok // SSM_C
    n_chunks = n_rows // bsz
    kern = functools.partial(_ssm_scan_kernel, bsz=bsz, n_chunks=n_chunks)
    return pl.pallas_call(
        kern,
        grid=(SSM_G // SCAN_GROUPS,),
        in_specs=[
            pl.BlockSpec((SCAN_GROUPS * SSM_P, n_tok), lambda i: (i, 0)),
            pl.BlockSpec((SCAN_GROUPS, SSM_K, SSM_K), lambda i: (i, 0, 0)),
            pl.BlockSpec((SCAN_GROUPS, SSM_K, 2 * SSM_N), lambda i: (i, 0, 0)),
            pl.BlockSpec((SCAN_GROUPS, 2 * SSM_N, SSM_K), lambda i: (i, 0, 0)),
            pl.BlockSpec((SCAN_GROUPS, 2, 2 * SSM_N), lambda i: (i, 0, 0)),
            pl.BlockSpec((SCAN_GROUPS * SSM_P, 1), lambda i: (i, 0)),
        ],
        out_specs=pl.BlockSpec((SCAN_GROUPS * SSM_P, n_tok), lambda i: (i, 0)),
        out_shape=jax.ShapeDtypeStruct((TOK_WIDTH, n_tok), F32),
        scratch_shapes=[pltpu.VMEM((SCAN_GROUPS, n_rows, 2 * SSM_N), F32)] * 3,
        compiler_params=_cparams(1),
        name="ssm_scan",
    )(zt, m, w, v, ac, d_col)


ATTN_BLOCK = 256
KV_SUB = 2


def _shared_kv_kernel(x_ref, g_ref, wk_ref, wv_ref, kg_ref, k_ref, vt_ref):
    h = _rms(x_ref[...], g_ref[...]).astype(BF16)
    k_ref[...] = _seg_rms64(_dot(h, wk_ref[...].astype(BF16)), kg_ref[...]).astype(BF16)
    v = _dot(h, wv_ref[...].astype(BF16))
    for j in range(vt_ref.shape[0]):
        vt_ref[j] = v[j * ATTN_BLOCK:(j + 1) * ATTN_BLOCK].T.astype(BF16)


def _shared_kv(x, g, wk, wv, k_gain, tm=1024):
    n_tok = x.shape[0]
    sub = tm // ATTN_BLOCK
    return pl.pallas_call(
        _shared_kv_kernel,
        grid=(n_tok // tm,),
        in_specs=[
            pl.BlockSpec((tm, D_MODEL), lambda i: (i, 0)),
            _const_spec((1, D_MODEL)),
            _const_spec(wk.shape),
            _const_spec(wv.shape),
            _const_spec((1, TOK_WIDTH)),
        ],
        out_specs=[pl.BlockSpec((tm, TOK_WIDTH), lambda i: (i, 0)),
                   pl.BlockSpec((sub, TOK_WIDTH, ATTN_BLOCK), lambda i: (i, 0, 0))],
        out_shape=[jax.ShapeDtypeStruct((n_tok, TOK_WIDTH), BF16),
                   jax.ShapeDtypeStruct((n_tok // ATTN_BLOCK, TOK_WIDTH, ATTN_BLOCK), BF16)],
        compiler_params=_cparams(1),
        name="shared_kv",
    )(x, g, wk, wv, k_gain)


N_STREAMS = 2 * DIFF_HEADS


def _diff_attn_kernel(qt_ref, k_ref, vt_ref, lam_ref, subg_ref, o_ref, qm_scr, sc_scr,
                      bm_scr, m_scr, acc_scr, *, lambda_init):
    tq = ATTN_BLOCK
    qi = pl.program_id(1)
    lp = lam_ref[...]
    lam = (jnp.exp(jnp.sum(lp[0:1] * lp[1:2], axis=-1, keepdims=True))
           - jnp.exp(jnp.sum(lp[2:3] * lp[3:4], axis=-1, keepdims=True)) + lambda_init)
    key_chunk = lax.broadcasted_iota(jnp.int32, (tq, tq), 0) // CHUNK
    qry_chunk = lax.broadcasted_iota(jnp.int32, (tq, tq), 1) // CHUNK
    diag_mask = key_chunk <= qry_chunk
    first_half = lax.broadcasted_iota(jnp.int32, (LANES, tq), 0) < DIFF_HEAD_DIM
    n_grp = DIFF_HEADS // 2

    for kg in range(2 * n_grp):
        qg = qt_ref[kg * LANES:(kg + 1) * LANES, :]
        zero = jnp.zeros_like(qg)
        qm_scr[kg, :, 0:tq] = jnp.where(first_half, qg, zero)
        qm_scr[kg, :, tq:2 * tq] = jnp.where(first_half, zero, qg)

    def block(j, diagonal, n_sub=1):
        nk = n_sub * tq
        rows = pl.ds(pl.multiple_of(j * tq, tq), nk)
        for kg in range(2 * n_grp):
            kb = k_ref[rows, kg * LANES:(kg + 1) * LANES]
            st = _dot(kb, qm_scr[kg])
            if diagonal:
                st = jnp.where(jnp.concatenate([diag_mask, diag_mask], axis=1), st, -jnp.inf)
            sc_scr[kg, 0:nk, :] = st
            bm_scr[kg] = jnp.max(st, axis=0, keepdims=True)
        for kg in range(2 * n_grp):
            g = kg % n_grp
            if diagonal:
                m_new = bm_scr[kg]
            else:
                m_old = m_scr[kg]
                m_new = jnp.maximum(m_old, bm_scr[kg])
                alpha = jnp.exp2(m_old - m_new)
            pb = jnp.exp2(sc_scr[kg, 0:nk, :] - m_new).astype(BF16)
            m_scr[kg] = m_new

            def v_rows(h):
                vt = [vt_ref[j + i, h * DIFF_V_DIM:(h + 1) * DIFF_V_DIM, :] for i in range(n_sub)]
                return _with_ones_rows(vt[0] if n_sub == 1 else jnp.concatenate(vt, axis=1))

            pv = jnp.concatenate(
                [_dot(v_rows(h), pb[:, hh * tq:(hh + 1) * tq])
                 for hh, h in enumerate((2 * g, 2 * g + 1))], axis=1)
            acc_scr[kg] = pv if diagonal else alpha * acc_scr[kg] + pv

    def body(t, carry):
        block(KV_SUB * t, False, KV_SUB)
        return carry

    block(qi, True)
    lax.fori_loop(0, qi // KV_SUB, body, 0)
    for r in range(1, KV_SUB):
        @pl.when(qi % KV_SUB >= r)
        def _():
            block(qi - qi % KV_SUB + r - 1, False)

    for head in range(DIFF_HEADS):
        rows = slice(head * DIFF_V_DIM, (head + 1) * DIFF_V_DIM)
        q_cols = slice((head % 2) * tq, (head % 2 + 1) * tq)
        on = []
        for comp in range(2):
            acc = acc_scr[comp * n_grp + head // 2]
            on.append(acc[0:DIFF_V_DIM, q_cols] * (1.0 / acc[DIFF_V_DIM:DIFF_V_DIM + 1, q_cols]))
        o = on[0] - lam * on[1]
        o = o * lax.rsqrt(jnp.mean(o * o, axis=0, keepdims=True) + EPS)
        o_ref[rows, :] = (o * subg_ref[rows, :] * (1.0 - lambda_init)).astype(o_ref.dtype)


def _diff_attn(qt, k, vt, lam_params, sub_gain, *, bsz, seq, lambda_init):
    tq = ATTN_BLOCK
    n_q = seq // tq
    return pl.pallas_call(
        functools.partial(_diff_attn_kernel, lambda_init=lambda_init),
        grid=(bsz, n_q),
        in_specs=[
            pl.BlockSpec((TOK_WIDTH, tq), lambda b, i: (0, b * n_q + i)),
            pl.BlockSpec((seq, TOK_WIDTH), lambda b, i: (b, 0)),
            pl.BlockSpec((n_q, TOK_WIDTH, tq), lambda b, i: (b, 0, 0)),
            pl.BlockSpec((4, DIFF_HEAD_DIM), lambda b, i: (0, 0)),
            pl.BlockSpec((TOK_WIDTH, 1), lambda b, i: (0, 0)),
        ],
        out_specs=pl.BlockSpec((TOK_WIDTH, tq), lambda b, i: (0, b * n_q + i)),
        out_shape=jax.ShapeDtypeStruct((TOK_WIDTH, bsz * seq), F32),
        scratch_shapes=[pltpu.VMEM((DIFF_HEADS, LANES, 2 * tq), BF16),
                        pltpu.VMEM((DIFF_HEADS, KV_SUB * tq, 2 * tq), F32),
                        pltpu.VMEM((DIFF_HEADS, 1, 2 * tq), F32),
                        pltpu.VMEM((DIFF_HEADS, 1, 2 * tq), F32),
                        pltpu.VMEM((DIFF_HEADS, DIFF_V_DIM + ONES_ROWS, 2 * tq), F32)],
        compiler_params=_cparams(2),
        name="diff_attn",
    )(qt, k, vt, lam_params, sub_gain)


def _gelu_tanh(x):
    return 0.5 * x * (1.0 + jnp.tanh(math.sqrt(2.0 / math.pi) * (x + 0.044715 * (x * x * x))))


def _sigmoid(x):
    return 1.0 / (1.0 + jnp.exp(-x))


POST_SUB_TILES = 2


def _post_kernel(tok_ref, omem_ref, x_ref, wglu_ref, wout_ref, g_ref, wg_ref, wu_ref,
                 wd_ref, o_ref, *, s5):
    sub = x_ref.shape[0] // POST_SUB_TILES
    fronts = []
    for i in range(POST_SUB_TILES):
        cols = slice(i * sub, (i + 1) * sub)
        tok = tok_ref[:, cols].T
        if s5:
            h = _gelu_tanh(tok)
            gate = _sigmoid(_dot(h.astype(BF16), wglu_ref[0].astype(BF16)))
            tok = h * gate
        mix = jnp.concatenate([tok.astype(BF16), omem_ref[:, cols].T.astype(BF16)], axis=1)
        x1 = x_ref[cols, :] + _dot(mix, wout_ref[0].astype(BF16))
        fronts.append((x1, _rms(x1, g_ref[...]).astype(BF16)))
    for i, (x1, h2) in enumerate(fronts):
        gate = _dot(h2, wg_ref[0])
        up = _dot(h2, wu_ref[0])
        act = (gate * _sigmoid(gate) * up).astype(BF16)
        o_ref[i * sub:(i + 1) * sub, :] = x1 + _dot(act, wd_ref[0])


def _post(tok, o_mem, x, w_glu, w_out, g, wg, wu, wd, *, layer, glu_layer, s5, tm=512):
    n_tok = x.shape[0]
    return pl.pallas_call(
        functools.partial(_post_kernel, s5=s5),
        grid=(n_tok // tm,),
        in_specs=[
            pl.BlockSpec((TOK_WIDTH, tm), lambda i: (0, i)),
            pl.BlockSpec((MEM_WIDTH, tm), lambda i: (0, i)),
            pl.BlockSpec((tm, D_MODEL), lambda i: (i, 0)),
            _layer_spec(w_glu.shape, glu_layer),
            _layer_spec(w_out.shape, layer),
            _const_spec((1, D_MODEL)),
            _layer_spec(wg.shape, layer),
            _layer_spec(wu.shape, layer),
            _layer_spec(wd.shape, layer),
        ],
        out_specs=pl.BlockSpec((tm, D_MODEL), lambda i: (i, 0)),
        out_shape=jax.ShapeDtypeStruct((n_tok, D_MODEL), F32),
        compiler_params=_cparams(1),
        name="post_s5" if s5 else "post",
    )(tok, o_mem, x, w_glu, w_out, g, wg, wu, wd)


def _tile_gain(g, reps):
    return jnp.tile(g.astype(F32), reps)[None, :]


def kernel(x, mem, norm_mix, norm_ffn, norm_mem, w_in, w_out, mem_wk, mem_wv, mem_q_norm,
           mem_k_norm, ffn_w_gate, ffn_w_up, ffn_w_down, ssm_a_re, ssm_a_im, ssm_log_dt,
           ssm_b_re, ssm_b_im, ssm_c_re, ssm_c_im, ssm_d, ssm_w_glu, kv_norm, diff_wk, diff_wv,
           diff_k_norm, diff_q_norm, diff_lambda_q1, diff_lambda_k1, diff_lambda_q2,
           diff_lambda_k2, diff_sub_norm):
    bsz, seq, _ = x.shape
    n_tok = bsz * seq
    xf = x.reshape(n_tok, D_MODEL)

    k_gain = jnp.tile(mem_k_norm.astype(F32), (1, MEM_HEADS))[:, None, :]
    k_mem, vt_mem = _mem_kv(mem, norm_mem[:, None, :], mem_wk, mem_wv, k_gain)

    ffn_gate, ffn_up, ffn_down = (w.astype(BF16) for w in (ffn_w_gate, ffn_w_up, ffn_w_down))

    k_sh = v_sh = None
    for layer in range(DEPTH):
        s5 = layer < N_A_LAYERS
        q_gain = _tile_gain(mem_q_norm[layer], MEM_HEADS).T
        if s5:
            tok_gain = jnp.ones((TOK_WIDTH, 1), F32)
        else:
            tok_gain = _tile_gain(diff_q_norm[layer - N_A_LAYERS], 2 * DIFF_HEADS).T
        z_tok, o_mem = _in_proj(xf, norm_mix[layer][None, :], w_in,
                                tok_gain, q_gain, k_mem, vt_mem,
                                layer=layer, seq=seq, qk_norm=not s5)
        if s5:
            i = layer
            log_dt = ssm_log_dt[i][:, None, None]
            a_row = jnp.tile(jnp.stack([ssm_a_re[i], ssm_a_im[i]], axis=1), (1, 1, 2))
            a_col = jnp.stack([ssm_a_re[i], ssm_a_im[i]], axis=2)
            b_t = jnp.tile(jnp.stack([ssm_b_re[i], ssm_b_im[i]], axis=1).transpose(0, 1, 3, 2),
                           (1, 1, 1, 2))
            c_t = jnp.stack([ssm_c_re[i], ssm_c_im[i]], axis=1).transpose(0, 1, 3, 2)
            m_op, w_op, v_op, ac = _ssm_prep(log_dt, a_row, a_col, b_t, c_t)
            tok = _ssm_scan(z_tok, m_op, w_op, v_op, ac, ssm_d[i][:, None], bsz=bsz)
        else:
            j = layer - N_A_LAYERS
            lambda_init = 0.8 - 0.6 * math.exp(-0.3 * layer)
            lam_params = jnp.stack([diff_lambda_q1[j], diff_lambda_k1[j],
                                    diff_lambda_q2[j], diff_lambda_k2[j]]).astype(F32)
            tok = _diff_attn(z_tok, k_sh, v_sh, lam_params,
                             _tile_gain(diff_sub_norm[j], DIFF_HEADS).T,
                             bsz=bsz, seq=seq, lambda_init=lambda_init)
        xf = _post(tok, o_mem, xf, ssm_w_glu, w_out, norm_ffn[layer][None, :],
                   ffn_gate, ffn_up, ffn_down,
                   layer=layer, glu_layer=layer if s5 else 0, s5=s5)
        if layer == N_A_LAYERS - 1:
            k_sh, v_sh = _shared_kv(xf, kv_norm[None, :], diff_wk, diff_wv,
                                    _tile_gain(diff_k_norm, 2 * DIFF_HEADS))
    return xf.reshape(bsz, seq, D_MODEL)
```

```python
import functools
import math

import jax
import jax.numpy as jnp
from jax import lax
from jax.experimental import pallas as pl
from jax.experimental.pallas import tpu as pltpu

D_MODEL = 1024
DEPTH = 4
CHUNK = 64
N_A_LAYERS = DEPTH // 2
MEM_HEADS = 4
MEM_HEAD_DIM = 64
MEM_WIDTH = MEM_HEADS * MEM_HEAD_DIM
TOK_WIDTH = D_MODEL - MEM_WIDTH
SSM_P = 16
SSM_G = TOK_WIDTH // SSM_P
SSM_N = 64
SSM_C = 64
SSM_K = SSM_P * SSM_C
SCAN_GROUPS = 4
DIFF_HEAD_DIM = 64
DIFF_HEADS = TOK_WIDTH // (2 * DIFF_HEAD_DIM)
DIFF_V_DIM = 2 * DIFF_HEAD_DIM
D_FF = 2816
EPS = 1e-6

LANES = 128
VMEM_LIMIT = 56 * 1024 * 1024

F32 = jnp.float32
BF16 = jnp.bfloat16


def _cparams(n_grid, vmem=VMEM_LIMIT, flags=None):
    return pltpu.CompilerParams(
        dimension_semantics=("arbitrary",) * n_grid, vmem_limit_bytes=vmem, flags=flags)


def _const_spec(shape):
    nd = len(shape)
    return pl.BlockSpec(shape, lambda *_: (0,) * nd, pipeline_mode=pl.Buffered(1))


def _layer_spec(shape, layer):
    nd = len(shape)
    return pl.BlockSpec((1,) + tuple(shape[1:]), lambda *_: (layer,) + (0,) * (nd - 1),
                        pipeline_mode=pl.Buffered(1))


def _rms(x, g):
    return x * lax.rsqrt(jnp.mean(x * x, axis=-1, keepdims=True) + EPS) * g


def _seg_rms64(x, g):
    seg = MEM_HEAD_DIM
    rows, width = x.shape
    lane = lax.broadcasted_iota(jnp.int32, (rows, LANES), 1)
    lo = lane < seg
    outs = []
    for v in range(width // LANES):
        xv = x[:, v * LANES:(v + 1) * LANES]
        sq = xv * xv
        s_lo = jnp.sum(jnp.where(lo, sq, 0.0), axis=-1, keepdims=True)
        s_hi = jnp.sum(jnp.where(lo, 0.0, sq), axis=-1, keepdims=True)
        r_lo = lax.rsqrt(s_lo * (1.0 / seg) + EPS)
        r_hi = lax.rsqrt(s_hi * (1.0 / seg) + EPS)
        outs.append(xv * jnp.where(lo, r_lo, r_hi) * g[:, v * LANES:(v + 1) * LANES])
    return jnp.concatenate(outs, axis=1)


def _dot(a, b):
    return jnp.dot(a, b, preferred_element_type=F32)


def _dot_nt(a, b):
    return lax.dot_general(a, b, (((1,), (1,)), ((), ())), preferred_element_type=F32)


def _mem_kv_kernel(mem_ref, g_ref, wk_ref, wv_ref, kg_ref, k_ref, vt_ref):
    bsz, n_mem, _ = mem_ref.shape
    mn = _rms(mem_ref[...].reshape(bsz * n_mem, D_MODEL), g_ref[0]).astype(BF16)
    k = _seg_rms64(_dot(mn, wk_ref[0].astype(BF16)), kg_ref[0])
    k_ref[0] = k.reshape(bsz, n_mem, MEM_WIDTH).astype(BF16)
    v = _dot(mn, wv_ref[0].astype(BF16))
    for b in range(bsz):
        vt_ref[0, b] = v[b * n_mem:(b + 1) * n_mem].T.astype(BF16)


def _mem_kv(mem, norm_mem, wk, wv, k_gain):
    bsz, n_mem, _ = mem.shape
    return pl.pallas_call(
        _mem_kv_kernel,
        grid=(DEPTH,),
        in_specs=[
            _const_spec(mem.shape),
            pl.BlockSpec((1, 1, D_MODEL), lambda l: (l, 0, 0)),
            pl.BlockSpec((1, D_MODEL, MEM_WIDTH), lambda l: (l, 0, 0)),
            pl.BlockSpec((1, D_MODEL, MEM_WIDTH), lambda l: (l, 0, 0)),
            pl.BlockSpec((1, 1, MEM_WIDTH), lambda l: (l, 0, 0)),
        ],
        out_specs=[pl.BlockSpec((1, bsz, n_mem, MEM_WIDTH), lambda l: (l, 0, 0, 0)),
                   pl.BlockSpec((1, bsz, MEM_WIDTH, n_mem), lambda l: (l, 0, 0, 0))],
        out_shape=[jax.ShapeDtypeStruct((DEPTH, bsz, n_mem, MEM_WIDTH), BF16),
                   jax.ShapeDtypeStruct((DEPTH, bsz, MEM_WIDTH, n_mem), BF16)],
        compiler_params=_cparams(1),
        name="mem_kv",
    )(mem, norm_mem, wk, wv, k_gain)


def _seg_rms_rows(xt, g_col, seg):
    width, cols = xt.shape
    x3 = xt.reshape(width // seg, seg, cols)
    ms = jnp.mean(x3 * x3, axis=1, keepdims=True)
    return (x3 * lax.rsqrt(ms + EPS)).reshape(width, cols) * g_col


ONES_ROWS = 16
SOFTMAX_LOG2E = math.log2(math.e)


def _with_ones_rows(vt):
    return jnp.concatenate([vt, jnp.ones((ONES_ROWS, vt.shape[1]), vt.dtype)], axis=0)


def _mem_attention_t(qt, k, vt):
    tm = qt.shape[1]
    first = lax.broadcasted_iota(jnp.int32, (LANES, tm), 0) < MEM_HEAD_DIM
    scores = []
    for grp in range(MEM_WIDTH // LANES):
        rows = slice(grp * LANES, (grp + 1) * LANES)
        qg = qt[rows, :]
        zero = jnp.zeros_like(qg)
        qm = jnp.concatenate([jnp.where(first, qg, zero), jnp.where(first, zero, qg)], axis=1)
        scores.append(_dot(k[:, rows], qm))
    probs = [jnp.exp2(s - jnp.max(s, axis=0, keepdims=True)).astype(BF16)
             for s in scores]
    outs = []
    for head in range(MEM_HEADS):
        v_rows = _with_ones_rows(vt[head * MEM_HEAD_DIM:(head + 1) * MEM_HEAD_DIM, :])
        o = _dot(v_rows, probs[head // 2][:, (head % 2) * tm:(head % 2 + 1) * tm])
        outs.append(o[:MEM_HEAD_DIM] * (1.0 / o[MEM_HEAD_DIM:MEM_HEAD_DIM + 1]))
    return jnp.concatenate(outs, axis=0)


def _in_proj_kernel(x_ref, g_ref, win_ref, tokg_ref, qg_ref, k_ref, vt_ref,
                    ztok_ref, omem_ref, wt_scr, *, qk_norm):
    @pl.when(pl.program_id(0) == 0)
    def _():
        wt_scr[...] = win_ref[0].T.astype(BF16)

    h = _rms(x_ref[...], g_ref[...]).astype(BF16)
    zt = _dot_nt(wt_scr[...], h)
    z_tok = zt[:TOK_WIDTH]
    if qk_norm:
        z_tok = (_seg_rms_rows(z_tok, tokg_ref[...], DIFF_HEAD_DIM)
                 * (DIFF_HEAD_DIM ** -0.5 * SOFTMAX_LOG2E))
    ztok_ref[...] = z_tok.astype(ztok_ref.dtype)
    q_mem = (_seg_rms_rows(zt[TOK_WIDTH:], qg_ref[...], MEM_HEAD_DIM)
             * (MEM_HEAD_DIM ** -0.5 * SOFTMAX_LOG2E))
    omem_ref[...] = _mem_attention_t(q_mem.astype(BF16), k_ref[0, 0],
                                     vt_ref[0, 0]).astype(omem_ref.dtype)


def _in_proj(x, g, w_in, tok_gain, q_gain, k_mem, vt_mem, *, layer, seq, qk_norm, tm=1024):
    n_tok = x.shape[0]
    n_mem = k_mem.shape[2]
    tm = min(tm, seq)
    assert seq % tm == 0
    tiles_per_seq = seq // tm
    return pl.pallas_call(
        functools.partial(_in_proj_kernel, qk_norm=qk_norm),
        grid=(n_tok // tm,),
        in_specs=[
            pl.BlockSpec((tm, D_MODEL), lambda i: (i, 0)),
            _const_spec((1, D_MODEL)),
            _layer_spec(w_in.shape, layer),
            _const_spec((TOK_WIDTH, 1)),
            _const_spec((MEM_WIDTH, 1)),
            pl.BlockSpec((1, 1, n_mem, MEM_WIDTH), lambda i: (layer, i // tiles_per_seq, 0, 0)),
            pl.BlockSpec((1, 1, MEM_WIDTH, n_mem), lambda i: (layer, i // tiles_per_seq, 0, 0)),
        ],
        out_specs=[pl.BlockSpec((TOK_WIDTH, tm), lambda i: (0, i)),
                   pl.BlockSpec((MEM_WIDTH, tm), lambda i: (0, i))],
        out_shape=[jax.ShapeDtypeStruct((TOK_WIDTH, n_tok), BF16 if qk_norm else F32),
                   jax.ShapeDtypeStruct((MEM_WIDTH, n_tok), F32)],
        scratch_shapes=[pltpu.VMEM((D_MODEL, D_MODEL), BF16)],
        compiler_params=_cparams(1),
        name="in_proj_q" if qk_norm else "in_proj_u",
    )(x, g, w_in, tok_gain, q_gain, k_mem, vt_mem)


def _cpow(ar, ai, dt, k):
    mag = jnp.exp(k * dt * ar)
    ang = k * dt * ai
    return mag * jnp.cos(ang), mag * jnp.sin(ang)


def _zoh_coef(ar, ai, abr, abi):
    den = ar * ar + ai * ai
    nr = abr - 1.0
    return (nr * ar + abi * ai) / den, (abi * ar - nr * ai) / den


def _ssm_prep_kernel(*refs):
    for gi in range(SCAN_GROUPS):
        _ssm_prep_group(*(r.at[pl.ds(gi, 1)] for r in refs))


def _ssm_prep_group(ldt_ref, arow_ref, acol_ref, bt_ref, ct_ref, m_ref, w_ref, v_ref, ac_ref):
    dt = jnp.exp(ldt_ref[0])
    first = lax.broadcasted_iota(jnp.int32, (1, LANES), 1) < SSM_N
    ar_r, ai_r = arow_ref[0, 0:1, :], arow_ref[0, 1:2, :]
    ar_c, ai_c = acol_ref[0, :, 0:1], acol_ref[0, :, 1:2]

    s_row = lax.broadcasted_iota(jnp.int32, (SSM_C, LANES), 0).astype(F32)
    e_re, e_im = _cpow(ar_r, ai_r, dt, float(SSM_C - 1) - s_row)
    ab_re, ab_im = e_re[SSM_C - 2:SSM_C - 1], e_im[SSM_C - 2:SSM_C - 1]
    cf_re, cf_im = _zoh_coef(ar_r, ai_r, ab_re, ab_im)

    pc_re = e_re[0:1] * ab_re - e_im[0:1] * ab_im
    pc_im = e_re[0:1] * ab_im + e_im[0:1] * ab_re
    ac_ref[0, 0:1, :] = pc_re
    ac_ref[0, 1:2, :] = jnp.where(first, -pc_im, pc_im)

    bre, bim = bt_ref[0, 0], bt_ref[0, 1]
    bb_re = cf_re * bre - cf_im * bim
    bb_im = cf_re * bim + cf_im * bre

    for p in range(SSM_P):
        br, bi = bb_re[p:p + 1, :], bb_im[p:p + 1, :]
        w_ref[0, p * SSM_C:(p + 1) * SSM_C, :] = jnp.where(
            first, br * e_re - bi * e_im, br * e_im + bi * e_re).astype(BF16)

    lane_n = lax.broadcasted_iota(jnp.int32, (SSM_N, LANES), 1)
    tau = (lane_n % SSM_C).astype(F32)
    p0_re, p0_im = _cpow(ar_c, ai_c, dt, tau)
    ab_re_c, ab_im_c = p0_re[:, 1:2], p0_im[:, 1:2]
    p1_re = p0_re * ab_re_c - p0_im * ab_im_c
    p1_im = p0_re * ab_im_c + p0_im * ab_re_c
    reps = SSM_K // LANES
    p0_re, p0_im = jnp.concatenate([p0_re] * reps, 1), jnp.concatenate([p0_im] * reps, 1)
    p1_re, p1_im = jnp.concatenate([p1_re] * reps, 1), jnp.concatenate([p1_im] * reps, 1)

    sel = (lax.broadcasted_iota(jnp.int32, (SSM_P, SSM_K), 1) // SSM_C
           == lax.broadcasted_iota(jnp.int32, (SSM_P, SSM_K), 0)).astype(BF16)

    def expand(ct):
        hi = ct.astype(BF16)
        rest = ct - hi.astype(F32)
        mid = rest.astype(BF16)
        lo = (rest - mid.astype(F32)).astype(BF16)
        return _dot(hi, sel) + _dot(mid, sel) + _dot(lo, sel)

    c_re, c_im = expand(ct_ref[0, 0]), expand(ct_ref[0, 1])

    v_ref[0, 0:SSM_N, :] = (c_re * p1_re - c_im * p1_im).astype(BF16)
    v_ref[0, SSM_N:2 * SSM_N, :] = (-(c_re * p1_im + c_im * p1_re)).astype(BF16)

    ce = jnp.concatenate([c_re * p0_re - c_im * p0_im, -(c_re * p0_im + c_im * p0_re)], axis=0)
    kt = jnp.dot(jnp.where(first, bb_re, bb_im), ce, precision=lax.Precision.HIGHEST,
                 preferred_element_type=F32)

    row = lax.broadcasted_iota(jnp.int32, (SSM_C, SSM_K), 0)
    lane_t = lax.broadcasted_iota(jnp.int32, (SSM_C, SSM_K), 1) % SSM_C
    causal = lane_t >= row
    for p in range(SSM_P):
        base = jnp.broadcast_to(kt[p:p + 1, :], (SSM_C, SSM_K))
        shifted = pltpu.roll(base, 0, 1, stride=1, stride_axis=0)
        m_ref[0, p * SSM_C:(p + 1) * SSM_C, :] = jnp.where(causal, shifted, 0.0).astype(BF16)


def _ssm_prep(log_dt, a_row, a_col, b_t, c_t):
    g = log_dt.shape[0]
    return pl.pallas_call(
        _ssm_prep_kernel,
        grid=(g // SCAN_GROUPS,),
        in_specs=[
            pl.BlockSpec((SCAN_GROUPS, 1, 1), lambda i: (i, 0, 0)),
            pl.BlockSpec((SCAN_GROUPS, 2, 2 * SSM_N), lambda i: (i, 0, 0)),
            pl.BlockSpec((SCAN_GROUPS, SSM_N, 2), lambda i: (i, 0, 0)),
            pl.BlockSpec((SCAN_GROUPS, 2, SSM_P, 2 * SSM_N), lambda i: (i, 0, 0, 0)),
            pl.BlockSpec((SCAN_GROUPS, 2, SSM_N, SSM_P), lambda i: (i, 0, 0, 0)),
        ],
        out_specs=[
            pl.BlockSpec((SCAN_GROUPS, SSM_K, SSM_K), lambda i: (i, 0, 0)),
            pl.BlockSpec((SCAN_GROUPS, SSM_K, 2 * SSM_N), lambda i: (i, 0, 0)),
            pl.BlockSpec((SCAN_GROUPS, 2 * SSM_N, SSM_K), lambda i: (i, 0, 0)),
            pl.BlockSpec((SCAN_GROUPS, 2, 2 * SSM_N), lambda i: (i, 0, 0)),
        ],
        out_shape=[
            jax.ShapeDtypeStruct((g, SSM_K, SSM_K), BF16),
            jax.ShapeDtypeStruct((g, SSM_K, 2 * SSM_N), BF16),
            jax.ShapeDtypeStruct((g, 2 * SSM_N, SSM_K), BF16),
            jax.ShapeDtypeStruct((g, 2, 2 * SSM_N), F32),
        ],
        compiler_params=_cparams(1),
        name="ssm_prep",
    )(log_dt, a_row, a_col, b_t, c_t)


SUBLANES = 8


def _transpose8(tiles):
    sub = lax.broadcasted_iota(jnp.int32, tiles[0].shape, 0)
    t = list(tiles)
    for d in (4, 2, 1):
        clear = (sub & d) == 0
        for i0 in range(SUBLANES):
            if i0 & d:
                continue
            lo, hi = t[i0], t[i0 + d]
            t[i0] = jnp.where(clear, lo, pltpu.roll(hi, d, 0))
            t[i0 + d] = jnp.where(clear, pltpu.roll(lo, SUBLANES - d, 0), hi)
    return t


def _rows_to_lane_blocks(x):
    n_rows, width = x.shape
    n = width // LANES
    out_rows = []
    for k in range(n // SUBLANES):
        pieces = []
        for b in range(n_rows // SUBLANES):
            xb = x[b * SUBLANES:(b + 1) * SUBLANES]
            pieces += _transpose8([xb[:, (SUBLANES * k + i) * LANES:(SUBLANES * k + i + 1) * LANES]
                                   for i in range(SUBLANES)])
        out_rows.append(jnp.concatenate(pieces, axis=1))
    return jnp.concatenate(out_rows, axis=0)


def _lane_blocks_to_rows(y, n_rows):
    n = y.shape[0]
    out_rows = []
    for b in range(n_rows // SUBLANES):
        pieces = []
        for k in range(n // SUBLANES):
            yk = y[k * SUBLANES:(k + 1) * SUBLANES]
            pieces += _transpose8([yk[:, (SUBLANES * b + j) * LANES:(SUBLANES * b + j + 1) * LANES]
                                   for j in range(SUBLANES)])
        out_rows.append(jnp.concatenate(pieces, axis=1))
    return jnp.concatenate(out_rows, axis=0)

def _ssm_scan_kernel(u_ref, m_ref, w_ref, v_ref, ac_ref, d_ref, y_ref, z_scr, zs_scr, sp_scr,
                     *, bsz, n_chunks):
    for gi in range(SCAN_GROUPS):
        ch = pl.ds(gi * SSM_P, SSM_P)
        one = pl.ds(gi, 1)
        _ssm_scan_group(u_ref.at[ch, :], m_ref.at[one], w_ref.at[one], v_ref.at[one],
                        ac_ref.at[one], d_ref.at[ch, :], y_ref.at[ch, :],
                        z_scr.at[gi], zs_scr.at[gi], sp_scr.at[gi], bsz=bsz, n_chunks=n_chunks)


def _ssm_scan_group(u_ref, m_ref, w_ref, v_ref, ac_ref, d_ref, y_ref, z_scr, zs_scr, sp_scr,
                    *, bsz, n_chunks):
    pairs = n_chunks // 2
    n_pairs = bsz * pairs
    ut = u_ref[...]
    u2 = _rows_to_lane_blocks(ut)
    low = lax.broadcasted_iota(jnp.int32, (n_pairs, LANES), 1) < SSM_C

    def col(a, j):
        return a[:, j * LANES:(j + 1) * LANES]

    def swap(a):
        return pltpu.roll(a, SSM_C, 1)

    even = [jnp.where(low, col(u2, 2 * k), swap(col(u2, 2 * k + 1))) for k in range(SSM_P // 2)]
    odd = [jnp.where(low, swap(col(u2, 2 * k)), col(u2, 2 * k + 1)) for k in range(SSM_P // 2)]
    ub = jnp.concatenate([jnp.concatenate(even, axis=1), jnp.concatenate(odd, axis=1)],
                         axis=0).astype(BF16)
    y = _dot(ub, m_ref[0])
    z = _dot(ub, w_ref[0])
    z_scr[...] = z
    zs_scr[...] = pltpu.roll(z, SSM_N, 1)

    a1 = ac_ref[0, 0:1, :]
    a2 = ac_ref[0, 1:2, :]
    s = jnp.zeros((bsz, 2 * SSM_N), F32)
    ss = jnp.zeros((bsz, 2 * SSM_N), F32)
    for c in range(n_chunks):
        rows = pl.ds((c % 2) * n_pairs + c // 2, bsz, stride=pairs)
        sp_scr[rows, :] = s
        if c + 1 < n_chunks:
            s, ss = (a1 * s + a2 * ss + z_scr[rows, :],
                     a1 * ss - a2 * s + zs_scr[rows, :])
    y = y + _dot(sp_scr[...].astype(BF16), v_ref[0])
    y_even, y_odd = y[:n_pairs], y[n_pairs:]
    cols = []
    for k in range(SSM_P // 2):
        ce, co = col(y_even, k), col(y_odd, k)
        cols += [jnp.where(low, ce, swap(co)), jnp.where(low, swap(ce), co)]
    y2 = jnp.concatenate(cols, axis=1)
    yt = _lane_blocks_to_rows(y2, SSM_P)
    y_ref[...] = yt + d_ref[...] * ut


def _ssm_scan(zt, m, w, v, ac, d_col, *, bsz):
    n_tok = zt.shape[1]
    n_rows = n_tok // SSM_C
    n_chunks = n_rows // bsz
    kern = functools.partial(_ssm_scan_kernel, bsz=bsz, n_chunks=n_chunks)
    return pl.pallas_call(
        kern,
        grid=(SSM_G // SCAN_GROUPS,),
        in_specs=[
            pl.BlockSpec((SCAN_GROUPS * SSM_P, n_tok), lambda i: (i, 0)),
            pl.BlockSpec((SCAN_GROUPS, SSM_K, SSM_K), lambda i: (i, 0, 0)),
            pl.BlockSpec((SCAN_GROUPS, SSM_K, 2 * SSM_N), lambda i: (i, 0, 0)),
            pl.BlockSpec((SCAN_GROUPS, 2 * SSM_N, SSM_K), lambda i: (i, 0, 0)),
            pl.BlockSpec((SCAN_GROUPS, 2, 2 * SSM_N), lambda i: (i, 0, 0)),
            pl.BlockSpec((SCAN_GROUPS * SSM_P, 1), lambda i: (i, 0)),
        ],
        out_specs=pl.BlockSpec((SCAN_GROUPS * SSM_P, n_tok), lambda i: (i, 0)),
        out_shape=jax.ShapeDtypeStruct((TOK_WIDTH, n_tok), F32),
        scratch_shapes=[pltpu.VMEM((SCAN_GROUPS, n_rows, 2 * SSM_N), F32)] * 3,
        compiler_params=_cparams(1),
        name="ssm_scan",
    )(zt, m, w, v, ac, d_col)


ATTN_BLOCK = 256
KV_SUB = 2


def _shared_kv_kernel(x_ref, g_ref, wk_ref, wv_ref, kg_ref, k_ref, vt_ref):
    h = _rms(x_ref[...], g_ref[...]).astype(BF16)
    k_ref[...] = _seg_rms64(_dot(h, wk_ref[...].astype(BF16)), kg_ref[...]).astype(BF16)
    v = _dot(h, wv_ref[...].astype(BF16))
    for j in range(vt_ref.shape[0]):
        vt_ref[j] = v[j * ATTN_BLOCK:(j + 1) * ATTN_BLOCK].T.astype(BF16)


def _shared_kv(x, g, wk, wv, k_gain, tm=1024):
    n_tok = x.shape[0]
    sub = tm // ATTN_BLOCK
    return pl.pallas_call(
        _shared_kv_kernel,
        grid=(n_tok // tm,),
        in_specs=[
            pl.BlockSpec((tm, D_MODEL), lambda i: (i, 0)),
            _const_spec((1, D_MODEL)),
            _const_spec(wk.shape),
            _const_spec(wv.shape),
            _const_spec((1, TOK_WIDTH)),
        ],
        out_specs=[pl.BlockSpec((tm, TOK_WIDTH), lambda i: (i, 0)),
                   pl.BlockSpec((sub, TOK_WIDTH, ATTN_BLOCK), lambda i: (i, 0, 0))],
        out_shape=[jax.ShapeDtypeStruct((n_tok, TOK_WIDTH), BF16),
                   jax.ShapeDtypeStruct((n_tok // ATTN_BLOCK, TOK_WIDTH, ATTN_BLOCK), BF16)],
        compiler_params=_cparams(1),
        name="shared_kv",
    )(x, g, wk, wv, k_gain)


def _diff_attn_kernel(qt_ref, k_ref, vt_ref, lam_ref, subg_ref, o_ref, qm_scr, sc_scr,
                      bm_scr, m_scr, acc_scr, *, lambda_init):
    tq = ATTN_BLOCK
    qi = pl.program_id(1)
    lp = lam_ref[...]
    lam = (jnp.exp(jnp.sum(lp[0:1] * lp[1:2], axis=-1, keepdims=True))
           - jnp.exp(jnp.sum(lp[2:3] * lp[3:4], axis=-1, keepdims=True)) + lambda_init)
    key_chunk = lax.broadcasted_iota(jnp.int32, (tq, tq), 0) // CHUNK
    qry_chunk = lax.broadcasted_iota(jnp.int32, (tq, tq), 1) // CHUNK
    diag_mask = key_chunk <= qry_chunk
    first_half = lax.broadcasted_iota(jnp.int32, (LANES, tq), 0) < DIFF_HEAD_DIM
    n_grp = DIFF_HEADS // 2

    for kg in range(2 * n_grp):
        qg = qt_ref[kg * LANES:(kg + 1) * LANES, :]
        zero = jnp.zeros_like(qg)
        qm_scr[kg, :, 0:tq] = jnp.where(first_half, qg, zero)
        qm_scr[kg, :, tq:2 * tq] = jnp.where(first_half, zero, qg)

    def block(j, diagonal, n_sub=1):
        nk = n_sub * tq
        rows = pl.ds(pl.multiple_of(j * tq, tq), nk)
        for kg in range(2 * n_grp):
            kb = k_ref[rows, kg * LANES:(kg + 1) * LANES]
            st = _dot(kb, qm_scr[kg])
            if diagonal:
                st = jnp.where(jnp.concatenate([diag_mask, diag_mask], axis=1), st, -jnp.inf)
            sc_scr[kg, 0:nk, :] = st
            bm_scr[kg] = jnp.max(st, axis=0, keepdims=True)
        for kg in range(2 * n_grp):
            g = kg % n_grp
            if diagonal:
                m_new = bm_scr[kg]
            else:
                m_old = m_scr[kg]
                m_new = jnp.maximum(m_old, bm_scr[kg])
                alpha = jnp.exp2(m_old - m_new)
            pb = jnp.exp2(sc_scr[kg, 0:nk, :] - m_new).astype(BF16)
            m_scr[kg] = m_new

            def v_rows(h):
                vt = [vt_ref[j + i, h * DIFF_V_DIM:(h + 1) * DIFF_V_DIM, :] for i in range(n_sub)]
                return _with_ones_rows(vt[0] if n_sub == 1 else jnp.concatenate(vt, axis=1))

            pv = jnp.concatenate(
                [_dot(v_rows(h), pb[:, hh * tq:(hh + 1) * tq])
                 for hh, h in enumerate((2 * g, 2 * g + 1))], axis=1)
            acc_scr[kg] = pv if diagonal else alpha * acc_scr[kg] + pv

    def body(t, carry):
        block(KV_SUB * t, False, KV_SUB)
        return carry

    block(qi, True)
    lax.fori_loop(0, qi // KV_SUB, body, 0)
    for r in range(1, KV_SUB):
        @pl.when(qi % KV_SUB >= r)
        def _():
            block(qi - qi % KV_SUB + r - 1, False)

    for head in range(DIFF_HEADS):
        rows = slice(head * DIFF_V_DIM, (head + 1) * DIFF_V_DIM)
        q_cols = slice((head % 2) * tq, (head % 2 + 1) * tq)
        on = []
        for comp in range(2):
            acc = acc_scr[comp * n_grp + head // 2]
            on.append(acc[0:DIFF_V_DIM, q_cols] * (1.0 / acc[DIFF_V_DIM:DIFF_V_DIM + 1, q_cols]))
        o = on[0] - lam * on[1]
        o = o * lax.rsqrt(jnp.mean(o * o, axis=0, keepdims=True) + EPS)
        o_ref[rows, :] = (o * subg_ref[rows, :] * (1.0 - lambda_init)).astype(o_ref.dtype)


def _diff_attn(qt, k, vt, lam_params, sub_gain, *, bsz, seq, lambda_init):
    tq = ATTN_BLOCK
    n_q = seq // tq
    return pl.pallas_call(
        functools.partial(_diff_attn_kernel, lambda_init=lambda_init),
        grid=(bsz, n_q),
        in_specs=[
            pl.BlockSpec((TOK_WIDTH, tq), lambda b, i: (0, b * n_q + i)),
            pl.BlockSpec((seq, TOK_WIDTH), lambda b, i: (b, 0)),
            pl.BlockSpec((n_q, TOK_WIDTH, tq), lambda b, i: (b, 0, 0)),
            pl.BlockSpec((4, DIFF_HEAD_DIM), lambda b, i: (0, 0)),
            pl.BlockSpec((TOK_WIDTH, 1), lambda b, i: (0, 0)),
        ],
        out_specs=pl.BlockSpec((TOK_WIDTH, tq), lambda b, i: (0, b * n_q + i)),
        out_shape=jax.ShapeDtypeStruct((TOK_WIDTH, bsz * seq), F32),
        scratch_shapes=[pltpu.VMEM((DIFF_HEADS, LANES, 2 * tq), BF16),
                        pltpu.VMEM((DIFF_HEADS, KV_SUB * tq, 2 * tq), F32),
                        pltpu.VMEM((DIFF_HEADS, 1, 2 * tq), F32),
                        pltpu.VMEM((DIFF_HEADS, 1, 2 * tq), F32),
                        pltpu.VMEM((DIFF_HEADS, DIFF_V_DIM + ONES_ROWS, 2 * tq), F32)],
        compiler_params=_cparams(2),
        name="diff_attn",
    )(qt, k, vt, lam_params, sub_gain)


def _gelu_tanh(x):
    return 0.5 * x * (1.0 + jnp.tanh(math.sqrt(2.0 / math.pi) * (x + 0.044715 * (x * x * x))))


def _sigmoid(x):
    return 1.0 / (1.0 + jnp.exp(-x))


POST_SUB_TILES = 2


def _post_kernel(tok_ref, omem_ref, x_ref, wglu_ref, wout_ref, g_ref, wg_ref, wu_ref,
                 wd_ref, o_ref, *, s5):
    sub = x_ref.shape[0] // POST_SUB_TILES
    fronts = []
    for i in range(POST_SUB_TILES):
        cols = slice(i * sub, (i + 1) * sub)
        tok = tok_ref[:, cols].T
        if s5:
            h = _gelu_tanh(tok)
            gate = _sigmoid(_dot(h.astype(BF16), wglu_ref[0].astype(BF16)))
            tok = h * gate
        mix = jnp.concatenate([tok.astype(BF16), omem_ref[:, cols].T.astype(BF16)], axis=1)
        x1 = x_ref[cols, :] + _dot(mix, wout_ref[0].astype(BF16))
        fronts.append((x1, _rms(x1, g_ref[...]).astype(BF16)))
    for i, (x1, h2) in enumerate(fronts):
        gate = _dot(h2, wg_ref[0])
        up = _dot(h2, wu_ref[0])
        act = (gate * _sigmoid(gate) * up).astype(BF16)
        o_ref[i * sub:(i + 1) * sub, :] = x1 + _dot(act, wd_ref[0])


def _post(tok, o_mem, x, w_glu, w_out, g, wg, wu, wd, *, layer, glu_layer, s5, tm=512):
    n_tok = x.shape[0]
    return pl.pallas_call(
        functools.partial(_post_kernel, s5=s5),
        grid=(n_tok // tm,),
        in_specs=[
            pl.BlockSpec((TOK_WIDTH, tm), lambda i: (0, i)),
            pl.BlockSpec((MEM_WIDTH, tm), lambda i: (0, i)),
            pl.BlockSpec((tm, D_MODEL), lambda i: (i, 0)),
            _layer_spec(w_glu.shape, glu_layer),
            _layer_spec(w_out.shape, layer),
            _const_spec((1, D_MODEL)),
            _layer_spec(wg.shape, layer),
            _layer_spec(wu.shape, layer),
            _layer_spec(wd.shape, layer),
        ],
        out_specs=pl.BlockSpec((tm, D_MODEL), lambda i: (i, 0)),
        out_shape=jax.ShapeDtypeStruct((n_tok, D_MODEL), F32),
        compiler_params=_cparams(1),
        name="post_s5" if s5 else "post",
    )(tok, o_mem, x, w_glu, w_out, g, wg, wu, wd)


def _tile_gain(g, reps):
    return jnp.tile(g.astype(F32), reps)[None, :]


def kernel(x, mem, norm_mix, norm_ffn, norm_mem, w_in, w_out, mem_wk, mem_wv, mem_q_norm,
           mem_k_norm, ffn_w_gate, ffn_w_up, ffn_w_down, ssm_a_re, ssm_a_im, ssm_log_dt,
           ssm_b_re, ssm_b_im, ssm_c_re, ssm_c_im, ssm_d, ssm_w_glu, kv_norm, diff_wk, diff_wv,
           diff_k_norm, diff_q_norm, diff_lambda_q1, diff_lambda_k1, diff_lambda_q2,
           diff_lambda_k2, diff_sub_norm):
    bsz, seq, d_model = x.shape
    n_tok = bsz * seq
    assert d_model == D_MODEL and seq % ATTN_BLOCK == 0 and seq % (2 * SSM_C) == 0
    assert (n_tok // LANES) % SUBLANES == 0
    xf = x.reshape(n_tok, D_MODEL)

    k_gain = jnp.tile(mem_k_norm.astype(F32), (1, MEM_HEADS))[:, None, :]
    k_mem, vt_mem = _mem_kv(mem, norm_mem[:, None, :], mem_wk, mem_wv, k_gain)

    ffn_gate, ffn_up, ffn_down = (w.astype(BF16) for w in (ffn_w_gate, ffn_w_up, ffn_w_down))

    k_sh = v_sh = None
    for layer in range(DEPTH):
        s5 = layer < N_A_LAYERS
        q_gain = _tile_gain(mem_q_norm[layer], MEM_HEADS).T
        if s5:
            tok_gain = jnp.ones((TOK_WIDTH, 1), F32)
        else:
            tok_gain = _tile_gain(diff_q_norm[layer - N_A_LAYERS], 2 * DIFF_HEADS).T
        z_tok, o_mem = _in_proj(xf, norm_mix[layer][None, :], w_in,
                                tok_gain, q_gain, k_mem, vt_mem,
                                layer=layer, seq=seq, qk_norm=not s5)
        if s5:
            i = layer
            log_dt = ssm_log_dt[i][:, None, None]
            a_row = jnp.tile(jnp.stack([ssm_a_re[i], ssm_a_im[i]], axis=1), (1, 1, 2))
            a_col = jnp.stack([ssm_a_re[i], ssm_a_im[i]], axis=2)
            b_t = jnp.tile(jnp.stack([ssm_b_re[i], ssm_b_im[i]], axis=1).transpose(0, 1, 3, 2),
                           (1, 1, 1, 2))
            c_t = jnp.stack([ssm_c_re[i], ssm_c_im[i]], axis=1).transpose(0, 1, 3, 2)
            m_op, w_op, v_op, ac = _ssm_prep(log_dt, a_row, a_col, b_t, c_t)
            tok = _ssm_scan(z_tok, m_op, w_op, v_op, ac, ssm_d[i][:, None], bsz=bsz)
        else:
            j = layer - N_A_LAYERS
            lambda_init = 0.8 - 0.6 * math.exp(-0.3 * layer)
            lam_params = jnp.stack([diff_lambda_q1[j], diff_lambda_k1[j],
                                    diff_lambda_q2[j], diff_lambda_k2[j]]).astype(F32)
            tok = _diff_attn(z_tok, k_sh, v_sh, lam_params,
                             _tile_gain(diff_sub_norm[j], DIFF_HEADS).T,
                             bsz=bsz, seq=seq, lambda_init=lambda_init)
        xf = _post(tok, o_mem, xf, ssm_w_glu, w_out, norm_ffn[layer][None, :],
                   ffn_gate, ffn_up, ffn_down,
                   layer=layer, glu_layer=layer if s5 else 0, s5=s5)
        if layer == N_A_LAYERS - 1:
            k_sh, v_sh = _shared_kv(xf, kv_norm[None, :], diff_wk, diff_wv,
                                    _tile_gain(diff_k_norm, 2 * DIFF_HEADS))
    return xf.reshape(bsz, seq, D_MODEL)
```
